```python
import math
import jax, jax.numpy as jnp
from jax import lax
import numpy as np

D_MODEL = 2048
BATCH = 32
SEQ = 256
DEPTH = 1
DEC_BATCH = 4
DEC_SEQ = 2048
PAST_LEN = 256

GRID_W = 64
H_A = 8
DQK_A = 64
DV_A = 128
W_A = H_A * DV_A
H_B = 8
DH_B = 64
DV_B = 2 * DH_B
W_B = H_B * DV_B
D_FF = 5632
CHUNK = 64
Q_BLOCK = 128
ROPE_BASE = 10000.0
GATE_CAP = 15.0
EPS = 1e-6
IN_SIZES = (H_A * DQK_A, H_A * DQK_A, W_A, W_A, 4 * H_A, H_B * 2 * DH_B, H_B * 2 * DH_B, W_B, D_MODEL, D_MODEL)
N_IN = 2 * H_A * DQK_A + 2 * W_A + 4 * H_A + 2 * H_B * 2 * DH_B + W_B + 2 * D_MODEL

kernel_name = 'diffusion_hybrid_mlstm_diffattn_step'


def rmsnorm(x, w):
    xf = x.astype(jnp.float32)
    y = xf * lax.rsqrt(jnp.mean(xf * xf, axis=-1, keepdims=True) + EPS)
    return (y * w.astype(jnp.float32)).astype(x.dtype)


def split_columns(a):
    idx = []
    acc = 0
    for s in IN_SIZES[:-1]:
        acc += s
        idx.append(acc)
    return jnp.split(a, idx, axis=-1)


def split_heads(x, n_heads):
    b, t, _ = x.shape
    return x.reshape(b, t, n_heads, -1).transpose(0, 2, 1, 3)


def merge_heads(x):
    b, h, t, d = x.shape
    return x.transpose(0, 2, 1, 3).reshape(b, t, h * d)


def axial_rope_tables(n_tokens):
    rows = n_tokens // GRID_W
    row = jnp.repeat(jnp.arange(rows), GRID_W).astype(jnp.float32)
    col = jnp.tile(jnp.arange(GRID_W), rows).astype(jnp.float32)
    quarter = DH_B // 4
    inv_freq = jnp.power(ROPE_BASE, -jnp.arange(quarter, dtype=jnp.float32) / quarter)
    ang_r = row[:, None] * inv_freq
    ang_c = col[:, None] * inv_freq
    ang = jnp.concatenate([ang_r, ang_r, ang_c, ang_c], axis=-1)
    return jnp.cos(ang), jnp.sin(ang)


def apply_axial_rope(x, cos, sin):
    x1, x2, x3, x4 = jnp.split(x, 4, axis=-1)
    rot = jnp.concatenate([-x2, x1, -x4, x3], axis=-1)
    return x * cos.astype(x.dtype) + rot * sin.astype(x.dtype)


def mlstm_scan(q, k, v, i_pre, log_f, C0, n0, m0):
    B, H, T, DK = q.shape
    nc = T // CHUNK
    f32 = jnp.float32

    def chunks(a):
        return jnp.moveaxis(a.reshape(a.shape[:2] + (nc, CHUNK) + a.shape[3:]), 2, 0)

    tri = jnp.tril(jnp.ones((CHUNK, CHUNK), dtype=bool))

    def step(carry, xs):
        C, n, m = carry
        qc, kc, vc, ic, fc = xs
        b = jnp.cumsum(fc, axis=-1)
        log_d = jnp.where(tri, b[..., :, None] - b[..., None, :] + ic[..., None, :], -jnp.inf)
        m_inter = b + m[..., None]
        m_t = jnp.maximum(m_inter, jnp.max(log_d, axis=-1))
        s = jnp.einsum('bhtd,bhsd->bhts', qc, kc) * jnp.exp(log_d - m_t[..., None])
        inter = jnp.exp(m_inter - m_t)
        num = jnp.einsum('bhts,bhsv->bhtv', s, vc) + inter[..., None] * jnp.einsum('bhtd,bhdv->bhtv', qc, C)
        den = jnp.sum(s, axis=-1) + inter * jnp.einsum('bhtd,bhd->bht', qc, n)
        h = num / jnp.maximum(jnp.abs(den), jnp.exp(-m_t))[..., None]
        b_last = b[..., -1]
        log_w = b_last[..., None] - b + ic
        m_new = jnp.maximum(b_last + m, jnp.max(log_w, axis=-1))
        w = jnp.exp(log_w - m_new[..., None])
        decay = jnp.exp(b_last + m - m_new)
        C = decay[..., None, None] * C + jnp.einsum('bhs,bhsd,bhsv->bhdv', w, kc, vc)
        n = decay[..., None] * n + jnp.einsum('bhs,bhsd->bhd', w, kc)
        return (C, n, m_new), h

    xs = tuple(chunks(a.astype(f32)) for a in (q, k, v, i_pre, log_f))
    (C, n, m), h = lax.scan(step, (C0.astype(f32), n0.astype(f32), m0.astype(f32)), xs)
    h = jnp.moveaxis(h, 0, 2).reshape(B, H, T, -1)
    return h, C, n, m


def mlstm_bidirectional(q, k, v, gates, C0, n0, m0):
    def flip(a):
        return jnp.flip(a, axis=2)
    h_f, Cf, nf, mf = mlstm_scan(q, k, v, gates[0], jax.nn.log_sigmoid(gates[1]), C0[:, 0], n0[:, 0], m0[:, 0])
    h_b, Cb, nb, mb = mlstm_scan(flip(q), flip(k), flip(v), flip(gates[2]), flip(jax.nn.log_sigmoid(gates[3])),
                                 C0[:, 1], n0[:, 1], m0[:, 1])
    h = h_f + flip(h_b)
    return h, jnp.stack([Cf, Cb], axis=1), jnp.stack([nf, nb], axis=1), jnp.stack([mf, mb], axis=1)


def diff_attention(q1, q2, k1, k2, v, lam):
    B, H, T, DH = q1.shape
    nb = T // Q_BLOCK
    scale = DH ** -0.5

    def to_blocks(a):
        return jnp.moveaxis(a.reshape(B, H, nb, Q_BLOCK, DH), 2, 0)

    def block(qs):
        qb1, qb2 = qs
        s1 = jnp.einsum('bhqd,bhkd->bhqk', qb1, k1).astype(jnp.float32) * scale
        s2 = jnp.einsum('bhqd,bhkd->bhqk', qb2, k2).astype(jnp.float32) * scale
        p = jax.nn.softmax(s1, axis=-1) - lam * jax.nn.softmax(s2, axis=-1)
        return jnp.einsum('bhqk,bhkv->bhqv', p.astype(v.dtype), v)

    o = lax.map(block, (to_blocks(q1), to_blocks(q2)))
    return jnp.moveaxis(o, 0, 2).reshape(B, H, T, -1)


def conv_ffn(h, w_up, conv_w, conv_b, w_down):
    u = h @ w_up
    up = jnp.pad(u, ((0, 0), (1, 1), (0, 0)))
    u = up[:, :-2] * conv_w[0] + up[:, 1:-1] * conv_w[1] + up[:, 2:] * conv_w[2] + conv_b
    a, b = jnp.split(u, 2, axis=-1)
    return (jax.nn.silu(a) * b) @ w_down


def token_mixers(h, p, lambda_init, rope, ctx_kv, state0):
    b, t, _ = h.shape
    q_a, k_a, v_a, o_a, g_pre, q_b, k_b, v_b, gate_a, gate_b = split_columns(h @ p['w_in'])
    qa = split_heads(q_a, H_A) * (DQK_A ** -0.5)
    ka = split_heads(k_a, H_A)
    va = split_heads(v_a, H_A)
    g = (g_pre + p['b_gates']).astype(jnp.float32)
    g = GATE_CAP * jnp.tanh(g / GATE_CAP)
    g = g.reshape(b, t, 4, H_A).transpose(2, 0, 3, 1)
    h_a, C, n, m = mlstm_bidirectional(qa, ka, va, g, state0[0], state0[1], state0[2])
    h_a = rmsnorm(h_a.astype(h.dtype), p['mlstm_norm'].reshape(H_A, 1, DV_A))
    y_a = merge_heads(h_a) * jax.nn.sigmoid(o_a)
    qb = split_heads(q_b, H_B)
    kb = split_heads(k_b, H_B)
    vb = split_heads(v_b, H_B)
    q1, q2 = qb[..., :DH_B], qb[..., DH_B:]
    if rope is None:
        keys, vals = kb, vb
    else:
        cos, sin = rope
        q1 = apply_axial_rope(q1, cos, sin)
        q2 = apply_axial_rope(q2, cos, sin)
        k_lat = jnp.concatenate([apply_axial_rope(kb[..., :DH_B], cos, sin),
                                 apply_axial_rope(kb[..., DH_B:], cos, sin)], axis=-1)
        keys = jnp.concatenate([k_lat, ctx_kv[0].astype(h.dtype)], axis=2)
        vals = jnp.concatenate([vb, ctx_kv[1].astype(h.dtype)], axis=2)
    f32 = jnp.float32
    lam = (jnp.exp(jnp.sum(p['lam_q1'].astype(f32) * p['lam_k1'].astype(f32)))
           - jnp.exp(jnp.sum(p['lam_q2'].astype(f32) * p['lam_k2'].astype(f32))) + lambda_init)
    o_b = diff_attention(q1, q2, keys[..., :DH_B], keys[..., DH_B:], vals, lam)
    o_b = rmsnorm(o_b, p['diff_norm']) * (1.0 - lambda_init)
    y_b = merge_heads(o_b)
    y = jax.nn.sigmoid(gate_a) * (y_a @ p['w_pa']) + jax.nn.sigmoid(gate_b) * (y_b @ p['w_pb'])
    return y @ p['w_out'], (kb, vb), (C, n, m)


def trunk_layer(x, mod, p, lambda_init, rope, ctx_kv, state0):
    shift1, scale1, gate1, shift2, scale2, gate2 = jnp.split(mod[:, None, :], 6, axis=-1)
    h = rmsnorm(x, p['norm1']) * (1.0 + scale1) + shift1
    y, kv, st = token_mixers(h, p, lambda_init, rope, ctx_kv, state0)
    x = x + gate1 * y
    h = rmsnorm(x, p['norm2']) * (1.0 + scale2) + shift2
    x = x + gate2 * conv_ffn(h, p['w_up'], p['conv_w'], p['conv_b'], p['w_down'])
    return x, kv, st


def setup_inputs(seed: int = 0) -> dict:
    key = jax.random.key(seed)
    ks = jax.random.split(key, 32)
    f32 = jnp.float32

    def nrm(k, shape, s):
        return jax.random.normal(k, shape, f32) * s

    gate_offset = jnp.array([0.0, 3.0, 0.0, 3.0], f32)[None, :, None]
    b_gates = (gate_offset + nrm(ks[12], (DEPTH, 4, H_A), 0.3)).reshape(DEPTH, 4 * H_A)
    return {
        'x_prompt': nrm(ks[0], (BATCH, SEQ, D_MODEL), 1.0),
        'x_sample': nrm(ks[1], (DEC_BATCH, DEC_SEQ, D_MODEL), 1.0),
        'cache_k': nrm(ks[2], (DEC_BATCH, DEPTH, H_B, PAST_LEN, 2 * DH_B), 1.0),
        'cache_v': nrm(ks[3], (DEC_BATCH, DEPTH, H_B, PAST_LEN, DV_B), 1.0),
        'state_C': nrm(ks[4], (DEC_BATCH, DEPTH, 2, H_A, DQK_A, DV_A), 0.5),
        'state_n': nrm(ks[5], (DEC_BATCH, DEPTH, 2, H_A, DQK_A), 0.5),
        'state_m': nrm(ks[6], (DEC_BATCH, DEPTH, 2, H_A), 1.0),
        'c': nrm(ks[7], (DEC_BATCH, D_MODEL), 1.0),
        'c_ctx': nrm(ks[8], (D_MODEL,), 1.0),
        'w_mod': nrm(ks[9], (DEPTH, D_MODEL, 6 * D_MODEL), 0.5 * D_MODEL ** -0.5),
        'b_mod': nrm(ks[10], (DEPTH, 6 * D_MODEL), 0.01),
        'norm1': 1.0 + nrm(ks[11], (DEPTH, D_MODEL), 0.01),
        'w_in': nrm(ks[13], (DEPTH, D_MODEL, N_IN), D_MODEL ** -0.5),
        'b_gates': b_gates,
        'mlstm_norm': 1.0 + nrm(ks[14], (DEPTH, W_A), 0.01),
        'lam_q1': nrm(ks[15], (DEPTH, DH_B), 0.1),
        'lam_k1': nrm(ks[16], (DEPTH, DH_B), 0.1),
        'lam_q2': nrm(ks[17], (DEPTH, DH_B), 0.1),
        'lam_k2': nrm(ks[18], (DEPTH, DH_B), 0.1),
        'diff_norm': 1.0 + nrm(ks[19], (DEPTH, DV_B), 0.01),
        'w_pa': nrm(ks[20], (DEPTH, W_A, D_MODEL), W_A ** -0.5),
        'w_pb': nrm(ks[21], (DEPTH, W_B, D_MODEL), W_B ** -0.5),
        'w_out': nrm(ks[22], (DEPTH, D_MODEL, D_MODEL), D_MODEL ** -0.5),
        'norm2': 1.0 + nrm(ks[23], (DEPTH, D_MODEL), 0.01),
        'w_up': nrm(ks[24], (DEPTH, D_MODEL, 2 * D_FF), D_MODEL ** -0.5),
        'conv_w': nrm(ks[25], (DEPTH, 3, 2 * D_FF), 3.0 ** -0.5),
        'conv_b': nrm(ks[26], (DEPTH, 2 * D_FF), 0.01),
        'w_down': nrm(ks[27], (DEPTH, D_FF, D_MODEL), D_FF ** -0.5),
        'final_norm': 1.0 + nrm(ks[28], (D_MODEL,), 0.01),
    }


def reference(x_prompt, x_sample, cache_k, cache_v, state_C, state_n, state_m, c, c_ctx,
              w_mod, b_mod, norm1, w_in, b_gates, mlstm_norm, lam_q1, lam_k1, lam_q2, lam_k2,
              diff_norm, w_pa, w_pb, w_out, norm2, w_up, conv_w, conv_b, w_down, final_norm):
    f32 = jnp.float32
    xp = x_prompt
    xs = x_sample
    bp = xp.shape[0]
    rope = axial_rope_tables(xs.shape[1])
    new_k, new_v, new_C, new_n, new_m = [], [], [], [], []
    for l in range(DEPTH):
        p = {'w_in': w_in[l], 'b_gates': b_gates[l], 'mlstm_norm': mlstm_norm[l],
             'lam_q1': lam_q1[l], 'lam_k1': lam_k1[l], 'lam_q2': lam_q2[l], 'lam_k2': lam_k2[l],
             'diff_norm': diff_norm[l], 'w_pa': w_pa[l], 'w_pb': w_pb[l], 'w_out': w_out[l],
             'norm1': norm1[l], 'norm2': norm2[l], 'w_up': w_up[l], 'conv_w': conv_w[l],
             'conv_b': conv_b[l], 'w_down': w_down[l]}
        lambda_init = 0.8 - 0.6 * math.exp(-0.3 * l)
        mod_ctx = jax.nn.silu(c_ctx)[None, :] @ w_mod[l] + b_mod[l]
        zero_state = (jnp.zeros((bp, 2, H_A, DQK_A, DV_A), f32),
                      jnp.zeros((bp, 2, H_A, DQK_A), f32),
                      jnp.zeros((bp, 2, H_A), f32))
        xp, (k_ctx, v_ctx), (C_ctx, n_ctx, m_ctx) = trunk_layer(xp, mod_ctx, p, lambda_init, None, None, zero_state)
        new_k.append(k_ctx.astype(x_prompt.dtype))
        new_v.append(v_ctx.astype(x_prompt.dtype))
        new_C.append(C_ctx.astype(x_prompt.dtype))
        new_n.append(n_ctx.astype(x_prompt.dtype))
        new_m.append(m_ctx.astype(x_prompt.dtype))
        mod_lat = jax.nn.silu(c) @ w_mod[l] + b_mod[l]
        xs, _, _ = trunk_layer(xs, mod_lat, p, lambda_init, rope, (cache_k[:, l], cache_v[:, l]),
                               (state_C[:, l], state_n[:, l], state_m[:, l]))
    y_prompt = rmsnorm(xp, final_norm)
    y_sample = rmsnorm(xs, final_norm)
    new_cache_k = jnp.stack(new_k, axis=1)
    new_cache_v = jnp.stack(new_v, axis=1)
    new_state_C = jnp.stack(new_C, axis=1)
    new_state_n = jnp.stack(new_n, axis=1)
    new_state_m = jnp.stack(new_m, axis=1)
    return (y_prompt, y_sample, new_cache_k, new_cache_v, new_state_C, new_state_n, new_state_m)
```

```python
import functools
import math

import jax
import jax.numpy as jnp
from jax import lax
from jax.experimental import pallas as pl
from jax.experimental.pallas import tpu as pltpu

F32 = jnp.float32
BF16 = jnp.bfloat16

GRID_W = 64
H_A = 8
DQK_A = 64
DV_A = 128
H_B = 8
DH_B = 64
DV_B = 128
ROPE_BASE = 10000.0
GATE_CAP = 15.0
EPS = 1e-6
NEG_BIG = -1e30

LANES = 128
BF16_ROWS = 16
VMEM_LIMIT = 56 * 1024 * 1024


def _sigmoid(x):
    return 1.0 / (1.0 + jnp.exp(-x))


def _dot(a, b):
    return jnp.dot(a, b, preferred_element_type=F32)


def _dot_nt(a, b):
    return lax.dot_general(a, b, (((1,), (1,)), ((), ())), preferred_element_type=F32)


def _dot_tn(a, b):
    return lax.dot_general(a, b, (((0,), (0,)), ((), ())), preferred_element_type=F32)


def _tile(n, want):
    t = min(want, n)
    while n % t:
        t -= LANES
    return t


def _params(*sem):
    return pltpu.CompilerParams(dimension_semantics=sem, vmem_limit_bytes=VMEM_LIMIT)


def _mod_kernel(c_ref, w_ref, b_ref, o_ref):
    c = c_ref[...]
    s = (c * _sigmoid(c)).astype(BF16)
    o_ref[...] = _dot(s, w_ref[...].astype(BF16)) + b_ref[...]


def _modulation(c_all, w_mod, b_mod, tn=1024):
    rows, d = c_all.shape
    n = w_mod.shape[1]
    tn = _tile(n, tn)
    return pl.pallas_call(
        _mod_kernel,
        grid=(n // tn,),
        in_specs=[pl.BlockSpec((rows, d), lambda j: (0, 0)),
                  pl.BlockSpec((d, tn), lambda j: (0, j)),
                  pl.BlockSpec((1, tn), lambda j: (0, j))],
        out_specs=pl.BlockSpec((rows, tn), lambda j: (0, j)),
        out_shape=jax.ShapeDtypeStruct((rows, n), F32),
        compiler_params=_params("arbitrary"),
        name="modulation",
    )(c_all, w_mod, b_mod)


def _modulated_norm(x, w, scale, shift):
    ms = jnp.mean(x * x, axis=-1, keepdims=True)
    return (x * lax.rsqrt(ms + EPS) * w) * (1.0 + scale) + shift


def _inproj_kernel(x_ref, mod_ref, n1_ref, w_ref, wg_ref, o_ref, g_ref, h_scr):
    @pl.when(pl.program_id(1) == 0)
    def _():
        h = _modulated_norm(x_ref[...], n1_ref[...], mod_ref[0, 1:2, :], mod_ref[0, 0:1, :])
        hb = h.astype(BF16)
        h_scr[...] = hb
        g_ref[...] = _dot(hb, wg_ref[...])

    o_ref[...] = _dot(h_scr[...], w_ref[...])


def _in_projection(x2d, mod3, mod_row, norm1, w_main, w_gates, tm=512, tn=1024):
    t, d = x2d.shape
    n = w_main.shape[1]
    ng = w_gates.shape[1]
    tn = _tile(n, tn)
    return pl.pallas_call(
        _inproj_kernel,
        grid=(t // tm, n // tn),
        in_specs=[pl.BlockSpec((tm, d), lambda i, j: (i, 0)),
                  pl.BlockSpec((1,) + mod3.shape[1:], lambda i, j: (mod_row(i * tm), 0, 0)),
                  pl.BlockSpec((1, d), lambda i, j: (0, 0)),
                  pl.BlockSpec((d, tn), lambda i, j: (0, j)),
                  pl.BlockSpec((d, ng), lambda i, j: (0, 0))],
        out_specs=[pl.BlockSpec((tm, tn), lambda i, j: (i, j)),
                   pl.BlockSpec((tm, ng), lambda i, j: (i, 0))],
        out_shape=[jax.ShapeDtypeStruct((t, n), F32), jax.ShapeDtypeStruct((t, ng), F32)],
        scratch_shapes=[pltpu.VMEM((tm, d), BF16)],
        compiler_params=_params("arbitrary", "arbitrary"),
        name="in_projection",
    )(x2d, mod3, norm1, w_main, w_gates)


def _split3(x):
    hi = x.astype(BF16)
    r = x - hi.astype(F32)
    mid = r.astype(BF16)
    lo = (r - mid.astype(F32)).astype(BF16)
    return hi, mid, lo


def _mlstm_kernel(*refs, seq, chunk, has_state, emit_state):
    q_ref, k_ref, v_ref, o_ref, g_ref, bg_ref, nrm_ref = refs[:7]
    pos = 7
    if has_state:
        c0_ref, n0_ref, m0_ref = refs[pos:pos + 3]
        pos += 3
    y_ref = refs[pos]
    pos += 1
    if emit_state:
        co_ref, no_ref, mo_ref = refs[pos:pos + 3]
        pos += 3
    bc_scr, a_scr, at_scr, hacc_scr = refs[pos:pos + 4]

    L = chunk
    nc = seq // L
    p = pl.program_id(1)

    @pl.when(p == 0)
    def _():
        row = lax.broadcasted_iota(jnp.int32, (L, L), 0)
        col = lax.broadcasted_iota(jnp.int32, (L, L), 1)
        tril = jnp.where(col <= row, 1.0, 0.0).astype(BF16)
        triu = jnp.where(col >= row, 1.0, 0.0).astype(BF16)
        lane = lax.broadcasted_iota(jnp.int32, (L, LANES), 1)
        for c in range(nc):
            rows = pl.ds(c * L, L)
            g = g_ref[rows, :] + bg_ref[...]
            g = GATE_CAP * jnp.tanh(g / GATE_CAP)
            gi = g[:, :LANES]
            gf = g[:, LANES:]
            lf = jnp.minimum(gf, 0.0) - jnp.log(1.0 + jnp.exp(-jnp.abs(gf)))
            hi, mid, lo = _split3(lf)
            pre = _dot(tril, hi) + _dot(tril, mid) + _dot(tril, lo)
            suf = _dot(triu, hi) + _dot(triu, mid) + _dot(triu, lo)
            bc = jnp.where(lane < H_A, pre, suf)
            a = gi - bc
            bc_scr[rows, :] = bc
            a_scr[rows, :] = a
            at_scr[c] = a.T
        if emit_state:
            mo_ref[...] = jnp.zeros(mo_ref.shape, F32)

    lane_row = lax.broadcasted_iota(jnp.int32, (1, LANES), 1)
    lane_l = lax.broadcasted_iota(jnp.int32, (L, LANES), 1)
    trow = lax.broadcasted_iota(jnp.int32, (L, L), 0)
    tcol = lax.broadcasted_iota(jnp.int32, (L, L), 1)
    m_rows = lax.broadcasted_iota(jnp.int32, (2, H_A), 0)
    m_cols = lax.broadcasted_iota(jnp.int32, (2, H_A), 1)

    n_final = [[None, None], [None, None]]
    for e in (0, 1):
        head = 2 * p + e
        hmask = (lane_row >= DQK_A) if e else (lane_row < DQK_A)
        vcols = slice(e * DV_A, (e + 1) * DV_A)
        for d in (0, 1):
            colidx = head + H_A * d
            causal = (tcol <= trow) if d == 0 else (tcol >= trow)
            sel_m = (m_rows == d) & (m_cols == head)

            if has_state:
                c0 = c0_ref[0, d, e]
                z = jnp.zeros_like(c0)
                cst0 = jnp.concatenate([z, c0] if e else [c0, z], axis=0)
                n0 = jnp.where(hmask, n0_ref[0, d, pl.ds(p, 1), :], 0.0)
                m0 = jnp.sum(jnp.sum(jnp.where(sel_m, m0_ref[0], 0.0), axis=1, keepdims=True),
                             axis=0, keepdims=True)
            else:
                cst0 = jnp.zeros((2 * DQK_A, DV_A), F32)
                n0 = jnp.zeros((1, LANES), F32)
                m0 = jnp.zeros((1, 1), F32)

            def chunk_step(ci, carry, e=e, d=d, hmask=hmask, vcols=vcols, colidx=colidx,
                           causal=causal):
                cst, nrow, m = carry
                c = ci if d == 0 else nc - 1 - ci
                r0 = c * L
                if not isinstance(r0, int):
                    r0 = pl.multiple_of(r0, L)
                rows = pl.ds(r0, L)
                qm = jnp.where(hmask, q_ref[rows, :] * (DQK_A ** -0.5), 0.0)
                qmb = qm.astype(BF16)
                kf = k_ref[rows, :]
                vh = v_ref[rows, vcols].astype(BF16)
                s = _dot_nt(qmb, kf.astype(BF16))
                pick = lane_l == colidx
                bcol = jnp.sum(jnp.where(pick, bc_scr[rows, :], 0.0), axis=1, keepdims=True)
                acol = jnp.sum(jnp.where(pick, a_scr[rows, :], 0.0), axis=1, keepdims=True)
                arow = at_scr[c, pl.ds(colidx, 1), :]
                logd = jnp.where(causal, bcol + arow, NEG_BIG)
                m_inter = bcol + m
                m_t = jnp.maximum(m_inter, jnp.max(logd, axis=1, keepdims=True))
                pmat = s * jnp.exp(logd - m_t)
                inter = jnp.exp(m_inter - m_t)
                num = _dot(pmat.astype(BF16), vh) + inter * _dot(qmb, cst.astype(BF16))
                den = (jnp.sum(pmat, axis=1, keepdims=True)
                       + inter * jnp.sum(qm * nrow, axis=1, keepdims=True))
                hh = num / jnp.maximum(jnp.abs(den), jnp.exp(-m_t))
                if d == 0:
                    hacc_scr[e, rows, :] = hh
                else:
                    ht = hacc_scr[e, rows, :] + hh
                    ms = jnp.mean(ht * ht, axis=1, keepdims=True)
                    hn = ht * lax.rsqrt(ms + EPS) * nrm_ref[:, vcols]
                    y_ref[rows, vcols] = (hn * _sigmoid(o_ref[rows, vcols])).astype(BF16)
                b_last = bcol[L - 1:L, :] if d == 0 else bcol[0:1, :]
                logw = b_last + acol
                m_new = jnp.maximum(b_last + m, jnp.max(logw, axis=0, keepdims=True))
                wk = jnp.exp(logw - m_new) * jnp.where(hmask, kf, 0.0)
                decay = jnp.exp(b_last + m - m_new)
                cst = decay * cst + _dot_tn(wk.astype(BF16), vh)
                nrow = decay * nrow + jnp.sum(wk, axis=0, keepdims=True)
                return cst, nrow, m_new

            if nc == 1:
                cst, nrow, m = chunk_step(0, (cst0, n0, m0))
            else:
                cst, nrow, m = lax.fori_loop(0, nc, chunk_step, (cst0, n0, m0))

            if emit_state:
                co_ref[0, d, e] = cst[e * DQK_A:(e + 1) * DQK_A, :]
                mo_ref[0] = jnp.where(sel_m, m, mo_ref[0])
                n_final[e][d] = nrow

    if emit_state:
        for d in (0, 1):
            no_ref[0, d, pl.ds(p, 1), :] = n_final[0][d] + n_final[1][d]


def _mlstm(proj, gates, b_gates2, mlstm_norm, batch, seq, chunk, state=None, emit_state=False):
    npairs = H_A // 2
    kblk = (H_A * DQK_A) // LANES
    vblk = (2 * H_A * DQK_A) // (2 * DV_A)
    oblk = vblk + (H_A * DV_A) // (2 * DV_A)
    ng = gates.shape[1]
    in_specs = [pl.BlockSpec((seq, LANES), lambda b, p: (b, p)),
                pl.BlockSpec((seq, LANES), lambda b, p: (b, kblk + p)),
                pl.BlockSpec((seq, 2 * DV_A), lambda b, p: (b, vblk + p)),
                pl.BlockSpec((seq, 2 * DV_A), lambda b, p: (b, oblk + p)),
                pl.BlockSpec((seq, ng), lambda b, p: (b, 0)),
                pl.BlockSpec((1, ng), lambda b, p: (0, 0)),
                pl.BlockSpec((1, 2 * DV_A), lambda b, p: (0, p))]
    args = [proj, proj, proj, proj, gates, b_gates2, mlstm_norm]
    if state is not None:
        c0, n0, m0 = state
        in_specs += [pl.BlockSpec((1, 2, 2, DQK_A, DV_A), lambda b, p: (b, 0, p, 0, 0)),
                     pl.BlockSpec((1, 2, npairs, LANES), lambda b, p: (b, 0, 0, 0)),
                     pl.BlockSpec((1, 2, H_A), lambda b, p: (b, 0, 0))]
        args += [c0, n0, m0]
    out_specs = [pl.BlockSpec((seq, 2 * DV_A), lambda b, p: (b, p))]
    out_shape = [jax.ShapeDtypeStruct((batch * seq, H_A * DV_A), BF16)]
    if emit_state:
        out_specs += [pl.BlockSpec((1, 2, 2, DQK_A, DV_A), lambda b, p: (b, 0, p, 0, 0)),
                      pl.BlockSpec((1, 2, npairs, LANES), lambda b, p: (b, 0, 0, 0)),
                      pl.BlockSpec((1, 2, H_A), lambda b, p: (b, 0, 0))]
        out_shape += [jax.ShapeDtypeStruct((batch, 2, H_A, DQK_A, DV_A), F32),
                      jax.ShapeDtypeStruct((batch, 2, npairs, LANES), F32),
                      jax.ShapeDtypeStruct((batch, 2, H_A), F32)]
    return pl.pallas_call(
        functools.partial(_mlstm_kernel, seq=seq, chunk=chunk, has_state=state is not None,
                          emit_state=emit_state),
        grid=(batch, npairs),
        in_specs=in_specs,
        out_specs=out_specs,
        out_shape=out_shape,
        scratch_shapes=[pltpu.VMEM((seq, LANES), F32), pltpu.VMEM((seq, LANES), F32),
                        pltpu.VMEM((seq // chunk, LANES, chunk), F32),
                        pltpu.VMEM((2, seq, DV_A), F32)],
        compiler_params=_params("arbitrary", "arbitrary"),
        name="mlstm",
    )(*args)


def _rope(x, cos, sin_signed, lane):
    ahead = pltpu.roll(x, LANES - DH_B // 4, axis=1)
    behind = pltpu.roll(x, DH_B // 4, axis=1)
    even_quarter = (lane // (DH_B // 4)) % 2 == 0
    return x * cos + jnp.where(even_quarter, ahead, behind) * sin_signed


def _attn_kernel(*refs, tq, seq, rope, lambda_init):
    if rope:
        (q_ref, k_ref, v_ref, ck_ref, cv_ref, cos_ref, sin_ref, lam_ref, dn_ref,
         o_ref, k_scr, v_scr) = refs
    else:
        q_ref, k_ref, v_ref, lam_ref, dn_ref, o_ref, nk_ref, nv_ref, k_scr, v_scr = refs
    qi = pl.program_id(2)
    lane = lax.broadcasted_iota(jnp.int32, (1, LANES), 1)

    @pl.when(qi == 0)
    def _():
        k = k_ref[...]
        v = v_ref[...]
        if rope:
            k_scr[0:seq, :] = _rope(k, cos_ref[...], sin_ref[...], lane).astype(BF16)
            v_scr[0:seq, :] = v.astype(BF16)
            k_scr[seq:, :] = ck_ref[0, 0].astype(BF16)
            v_scr[seq:, :] = cv_ref[0, 0].astype(BF16)
        else:
            k_scr[...] = k.astype(BF16)
            v_scr[...] = v.astype(BF16)
            nk_ref[0, 0] = k
            nv_ref[0, 0] = v

    q = q_ref[...]
    if rope:
        rows = pl.ds(pl.multiple_of(qi * tq, tq), tq)
        q = _rope(q, cos_ref[rows, :], sin_ref[rows, :], lane)
    q = q * (DH_B ** -0.5)
    q1 = jnp.where(lane < DH_B, q, 0.0).astype(BF16)
    q2 = jnp.where(lane >= DH_B, q, 0.0).astype(BF16)
    keys = k_scr[...]

    def softmax(qh):
        s = _dot_nt(qh, keys)
        ex = jnp.exp(s - jnp.max(s, axis=1, keepdims=True))
        return ex * (1.0 / jnp.sum(ex, axis=1, keepdims=True))

    lp = lam_ref[...]
    lam = (jnp.exp(jnp.sum(lp[0:1] * lp[1:2], axis=1, keepdims=True))
           - jnp.exp(jnp.sum(lp[2:3] * lp[3:4], axis=1, keepdims=True)) + lambda_init)
    pm = softmax(q1) - lam * softmax(q2)
    o = _dot(pm.astype(BF16), v_scr[...])
    ms = jnp.mean(o * o, axis=1, keepdims=True)
    o = (o * lax.rsqrt(ms + EPS) * dn_ref[...]) * (1.0 - lambda_init)
    o_ref[...] = o.astype(BF16)


def _diff_attention(proj, lam_params, diff_norm, batch, seq, tq, lambda_init, cache=None,
                    rope_tables=None):
    qblk = (2 * H_A * DQK_A + 2 * H_A * DV_A) // LANES
    kblk = qblk + H_B
    vblk = kblk + H_B
    nq = seq // tq
    rope = cache is not None
    in_specs = [pl.BlockSpec((tq, LANES), lambda b, h, i: (b * nq + i, qblk + h)),
                pl.BlockSpec((seq, LANES), lambda b, h, i: (b, kblk + h)),
                pl.BlockSpec((seq, LANES), lambda b, h, i: (b, vblk + h))]
    args = [proj, proj, proj]
    tk = seq
    if rope:
        ck, cv = cache
        past = ck.shape[2]
        tk = seq + past
        cos, sin_signed = rope_tables
        in_specs += [pl.BlockSpec((1, 1, past, LANES), lambda b, h, i: (b, h, 0, 0)),
                     pl.BlockSpec((1, 1, past, LANES), lambda b, h, i: (b, h, 0, 0)),
                     pl.BlockSpec((seq, LANES), lambda b, h, i: (0, 0)),
                     pl.BlockSpec((seq, LANES), lambda b, h, i: (0, 0))]
        args += [ck, cv, cos, sin_signed]
    in_specs += [pl.BlockSpec(lam_params.shape, lambda b, h, i: (0, 0)),
                 pl.BlockSpec((1, DV_B), lambda b, h, i: (0, 0))]
    args += [lam_params, diff_norm]
    out_specs = [pl.BlockSpec((tq, DV_B), lambda b, h, i: (b * nq + i, h))]
    out_shape = [jax.ShapeDtypeStruct((batch * seq, H_B * DV_B), BF16)]
    if not rope:
        out_specs += [pl.BlockSpec((1, 1, seq, LANES), lambda b, h, i: (b, h, 0, 0)),
                      pl.BlockSpec((1, 1, seq, LANES), lambda b, h, i: (b, h, 0, 0))]
        out_shape += [jax.ShapeDtypeStruct((batch, H_B, seq, LANES), F32),
                      jax.ShapeDtypeStruct((batch, H_B, seq, DV_B), F32)]
    return pl.pallas_call(
        functools.partial(_attn_kernel, tq=tq, seq=seq, rope=rope, lambda_init=lambda_init),
        grid=(batch, H_B, nq),
        in_specs=in_specs,
        out_specs=out_specs,
        out_shape=out_shape,
        scratch_shapes=[pltpu.VMEM((tk, LANES), BF16), pltpu.VMEM((tk, DV_B), BF16)],
        compiler_params=_params("arbitrary", "arbitrary", "arbitrary"),
        name="diff_attention",
    )(*args)


def _merge_kernel(ya_ref, yb_ref, ga_ref, gb_ref, x_ref, mod_ref, wpa_ref, wpb_ref, wo_ref, o_ref):
    ya = _dot(ya_ref[...], wpa_ref[...])
    yb = _dot(yb_ref[...], wpb_ref[...])
    y = _sigmoid(ga_ref[...]) * ya + _sigmoid(gb_ref[...]) * yb
    z = _dot(y.astype(BF16), wo_ref[...])
    o_ref[...] = x_ref[...] + mod_ref[0, 2:3, :] * z


def _merge_project(y_a, y_b, proj, x2d, mod3, mod_row, w_pa, w_pb, w_out, tm=256):
    t, d = x2d.shape
    wa = y_a.shape[1]
    wb = y_b.shape[1]
    gablk = (proj.shape[1] - 2 * d) // d
    const = dict(pipeline_mode=pl.Buffered(1))
    return pl.pallas_call(
        _merge_kernel,
        grid=(t // tm,),
        in_specs=[pl.BlockSpec((tm, wa), lambda i: (i, 0)),
                  pl.BlockSpec((tm, wb), lambda i: (i, 0)),
                  pl.BlockSpec((tm, d), lambda i: (i, gablk)),
                  pl.BlockSpec((tm, d), lambda i: (i, gablk + 1)),
                  pl.BlockSpec((tm, d), lambda i: (i, 0)),
                  pl.BlockSpec((1,) + mod3.shape[1:], lambda i: (mod_row(i * tm), 0, 0)),
                  pl.BlockSpec((wa, d), lambda i: (0, 0), **const),
                  pl.BlockSpec((wb, d), lambda i: (0, 0), **const),
                  pl.BlockSpec((d, d), lambda i: (0, 0), **const)],
        out_specs=pl.BlockSpec((tm, d), lambda i: (i, 0)),
        out_shape=jax.ShapeDtypeStruct((t, d), F32),
        compiler_params=_params("arbitrary"),
        name="merge_project",
    )(y_a, y_b, proj, proj, x2d, mod3, w_pa, w_pb, w_out)


def _ffn_kernel(x_ref, xp_ref, xn_ref, mod_ref, n2_ref, wa_ref, wb_ref, cwa_ref, cwb_ref,
                cba_ref, cbb_ref, wd_ref, fn_ref, o_ref, h_scr, acc_scr, ua_scr, ub_scr,
                *, tm, seq):
    i = pl.program_id(0)
    j = pl.program_id(1)
    halo = BF16_ROWS

    @pl.when(j == 0)
    def _():
        def nrm(x):
            return _modulated_norm(x, n2_ref[...], mod_ref[0, 4:5, :], mod_ref[0, 3:4, :])
        h_scr[0:halo, :] = nrm(xp_ref[...]).astype(BF16)
        h_scr[halo:halo + tm, :] = nrm(x_ref[...]).astype(BF16)
        h_scr[halo + tm:, :] = nrm(xn_ref[...]).astype(BF16)
        acc_scr[...] = jnp.zeros(acc_scr.shape, F32)

    h = h_scr[...]
    ua_scr[...] = _dot(h, wa_ref[...])
    ub_scr[...] = _dot(h, wb_ref[...])

    pos = (i * tm + lax.broadcasted_iota(jnp.int32, (tm, 1), 0)) % seq
    has_prev = jnp.where(pos != 0, 1.0, 0.0)
    has_next = jnp.where(pos != seq - 1, 1.0, 0.0)

    def conv(u_scr, cw_ref, cb_ref):
        prev = u_scr[halo - 1:halo - 1 + tm, :] * has_prev
        cur = u_scr[halo:halo + tm, :]
        nxt = u_scr[halo + 1:halo + 1 + tm, :] * has_next
        return prev * cw_ref[0:1, :] + cur * cw_ref[1:2, :] + nxt * cw_ref[2:3, :] + cb_ref[...]

    a = conv(ua_scr, cwa_ref, cba_ref)
    b = conv(ub_scr, cwb_ref, cbb_ref)
    g = (a * _sigmoid(a)) * b
    acc_scr[...] += _dot(g.astype(BF16), wd_ref[...])

    @pl.when(j == pl.num_programs(1) - 1)
    def _():
        x2 = x_ref[...] + mod_ref[0, 5:6, :] * acc_scr[...]
        ms = jnp.mean(x2 * x2, axis=-1, keepdims=True)
        o_ref[...] = x2 * lax.rsqrt(ms + EPS) * fn_ref[...]


def _conv_ffn(x2d, mod3, mod_row, norm2, w_up, conv_w, conv_b, w_down, final_norm, seq,
              tm=512, tf=512):
    t, d = x2d.shape
    dff = w_down.shape[0]
    nf = dff // tf
    halo = BF16_ROWS
    nhalo = t // halo
    per = tm // halo
    return pl.pallas_call(
        functools.partial(_ffn_kernel, tm=tm, seq=seq),
        grid=(t // tm, nf),
        in_specs=[pl.BlockSpec((tm, d), lambda i, j: (i, 0)),
                  pl.BlockSpec((halo, d), lambda i, j: (jnp.maximum(i * per - 1, 0), 0)),
                  pl.BlockSpec((halo, d), lambda i, j: (jnp.minimum((i + 1) * per, nhalo - 1), 0)),
                  pl.BlockSpec((1,) + mod3.shape[1:], lambda i, j: (mod_row(i * tm), 0, 0)),
                  pl.BlockSpec((1, d), lambda i, j: (0, 0)),
                  pl.BlockSpec((d, tf), lambda i, j: (0, j)),
                  pl.BlockSpec((d, tf), lambda i, j: (0, nf + j)),
                  pl.BlockSpec((3, tf), lambda i, j: (0, j)),
                  pl.BlockSpec((3, tf), lambda i, j: (0, nf + j)),
                  pl.BlockSpec((1, tf), lambda i, j: (0, j)),
                  pl.BlockSpec((1, tf), lambda i, j: (0, nf + j)),
                  pl.BlockSpec((tf, d), lambda i, j: (j, 0)),
                  pl.BlockSpec((1, d), lambda i, j: (0, 0))],
        out_specs=pl.BlockSpec((tm, d), lambda i, j: (i, 0)),
        out_shape=jax.ShapeDtypeStruct((t, d), F32),
        scratch_shapes=[pltpu.VMEM((tm + 2 * halo, d), BF16), pltpu.VMEM((tm, d), F32),
                        pltpu.VMEM((tm + 2 * halo, tf), F32), pltpu.VMEM((tm + 2 * halo, tf), F32)],
        compiler_params=_params("arbitrary", "arbitrary"),
        name="conv_ffn",
    )(x2d, x2d, x2d, mod3, norm2, w_up, w_up, conv_w, conv_w, conv_b, conv_b, w_down, final_norm)


def _rope_tables(n_tokens):
    rows = n_tokens // GRID_W
    row = jnp.repeat(jnp.arange(rows), GRID_W).astype(F32)
    col = jnp.tile(jnp.arange(GRID_W), rows).astype(F32)
    quarter = DH_B // 4
    inv_freq = jnp.power(ROPE_BASE, -jnp.arange(quarter, dtype=F32) / quarter)
    ang_r = row[:, None] * inv_freq
    ang_c = col[:, None] * inv_freq
    ang = jnp.concatenate([ang_r, ang_r, ang_c, ang_c] * 2, axis=-1)
    sign = jnp.where((jnp.arange(2 * DH_B) // quarter) % 2 == 0, -1.0, 1.0).astype(F32)
    return jnp.cos(ang), jnp.sin(ang) * sign


def _gate_columns(g):
    lead = g.shape[:-1]
    g4 = g.reshape(lead + (4, H_A))
    pad = jnp.zeros(lead + (LANES - 2 * H_A,), g.dtype)
    return jnp.concatenate([g4[..., 0, :], g4[..., 2, :], pad, g4[..., 1, :], g4[..., 3, :], pad],
                           axis=-1)


def _trunk(x2d, batch, seq, mod3, mod_row, wts, lambda_init, chunk, tq, cache=None, state=None,
           rope_tables=None, emit_state=False):
    proj, gates = _in_projection(x2d, mod3, mod_row, wts["norm1"], wts["w_main"], wts["w_gates"])
    a_out = _mlstm(proj, gates, wts["b_gates"], wts["mlstm_norm"], batch, seq, chunk,
                   state=state, emit_state=emit_state)
    b_out = _diff_attention(proj, wts["lam"], wts["diff_norm"], batch, seq, tq, lambda_init,
                            cache=cache, rope_tables=rope_tables)
    x1 = _merge_project(a_out[0], b_out[0], proj, x2d, mod3, mod_row, wts["w_pa"], wts["w_pb"],
                        wts["w_out"])
    y = _conv_ffn(x1, mod3, mod_row, wts["norm2"], wts["w_up"], wts["conv_w"], wts["conv_b"],
                  wts["w_down"], wts["final_norm"], seq)
    return y, a_out[1:], b_out[1:]


def kernel(x_prompt, x_sample, cache_k, cache_v, state_C, state_n, state_m, c, c_ctx, w_mod, b_mod, norm1, w_in, b_gates, mlstm_norm, lam_q1, lam_k1, lam_q2, lam_k2, diff_norm, w_pa, w_pb, w_out, norm2, w_up, conv_w, conv_b, w_down, final_norm):
    assert w_in.shape[0] == 1, "single trunk layer"
    bp, sp, d = x_prompt.shape
    bs, ss, _ = x_sample.shape
    lambda_init = 0.8 - 0.6 * math.exp(-0.3 * 0)

    g0 = 2 * H_A * DQK_A + 2 * H_A * DV_A
    w = w_in[0]
    wts = {
        "w_main": jnp.concatenate([w[:, :g0], w[:, g0 + 4 * H_A:]], axis=1).astype(BF16),
        "w_gates": _gate_columns(w[:, g0:g0 + 4 * H_A]).astype(BF16),
        "b_gates": _gate_columns(b_gates[0])[None, :],
        "norm1": norm1, "norm2": norm2, "mlstm_norm": mlstm_norm, "diff_norm": diff_norm,
        "lam": jnp.concatenate([lam_q1, lam_k1, lam_q2, lam_k2], axis=0),
        "w_pa": w_pa[0].astype(BF16), "w_pb": w_pb[0].astype(BF16), "w_out": w_out[0].astype(BF16),
        "w_up": w_up[0].astype(BF16), "conv_w": conv_w[0], "conv_b": conv_b,
        "w_down": w_down[0].astype(BF16), "final_norm": final_norm[None, :],
    }

    c_all = jnp.concatenate([c_ctx[None, :], c, jnp.zeros((8 - 1 - bs, d), F32)], axis=0)
    mod3 = _modulation(c_all, w_mod[0], b_mod).reshape(8, 6, d)

    yp, (c_new, n_new, m_new), (k_new, v_new) = _trunk(
        x_prompt.reshape(bp * sp, d), bp, sp, mod3, lambda t: 0, wts, lambda_init,
        chunk=sp, tq=sp, emit_state=True)

    state = (state_C[:, 0], state_n[:, 0].reshape(bs, 2, H_A // 2, LANES), state_m[:, 0])
    ys, _, _ = _trunk(
        x_sample.reshape(bs * ss, d), bs, ss, mod3, lambda t: 1 + t // ss, wts, lambda_init,
        chunk=256, tq=256, cache=(cache_k[:, 0], cache_v[:, 0]), state=state,
        rope_tables=_rope_tables(ss))

    return (yp.reshape(bp, sp, d), ys.reshape(bs, ss, d),
            k_new[:, None], v_new[:, None], c_new[:, None],
            n_new.reshape(bp, 1, 2, H_A, DQK_A), m_new[:, None])
```

```python
import functools
import math

import jax
import jax.numpy as jnp
from jax import lax
from jax.experimental import pallas as pl
from jax.experimental.pallas import tpu as pltpu

F32 = jnp.float32
BF16 = jnp.bfloat16

GRID_W = 64
H_A = 8
DQK_A = 64
DV_A = 128
H_B = 8
DH_B = 64
DV_B = 128
ROPE_BASE = 10000.0
GATE_CAP = 15.0
EPS = 1e-6
NEG_BIG = -1e30
LOG2_E = 1.4426950408889634

LANES = 128
BF16_ROWS = 16
VMEM_LIMIT = 56 * 1024 * 1024


def _sigmoid(x):
    return 1.0 / (1.0 + jnp.exp(-x))


def _dot(a, b):
    return jnp.dot(a, b, preferred_element_type=F32)


def _dot_nt(a, b):
    return lax.dot_general(a, b, (((1,), (1,)), ((), ())), preferred_element_type=F32)


def _dot_tn(a, b):
    return lax.dot_general(a, b, (((0,), (0,)), ((), ())), preferred_element_type=F32)


def _tile(n, want):
    t = min(want, n)
    while n % t:
        t -= LANES
    return t


def _params(*sem):
    return pltpu.CompilerParams(dimension_semantics=sem, vmem_limit_bytes=VMEM_LIMIT)


def _mod_kernel(c_ref, w_ref, b_ref, o_ref):
    c = c_ref[...]
    s = (c * _sigmoid(c)).astype(BF16)
    o_ref[...] = _dot(s, w_ref[...].astype(BF16)) + b_ref[...]


def _modulation(c_all, w_mod, b_mod, tn=1024):
    rows, d = c_all.shape
    n = w_mod.shape[1]
    tn = _tile(n, tn)
    return pl.pallas_call(
        _mod_kernel,
        grid=(n // tn,),
        in_specs=[pl.BlockSpec((rows, d), lambda j: (0, 0)),
                  pl.BlockSpec((d, tn), lambda j: (0, j)),
                  pl.BlockSpec((1, tn), lambda j: (0, j))],
        out_specs=pl.BlockSpec((rows, tn), lambda j: (0, j)),
        out_shape=jax.ShapeDtypeStruct((rows, n), F32),
        compiler_params=_params("arbitrary"),
        name="modulation",
    )(c_all, w_mod, b_mod)


def _modulated_norm(x, w, scale, shift):
    ms = jnp.mean(x * x, axis=-1, keepdims=True)
    return (x * lax.rsqrt(ms + EPS) * w) * (1.0 + scale) + shift


def _inproj_kernel(x_ref, mod_ref, n1_ref, w_ref, wg_ref, o16_ref, o32_ref, g_ref, h_scr, *, n16):
    j = pl.program_id(1)

    @pl.when(j == 0)
    def _():
        h = _modulated_norm(x_ref[...], n1_ref[...], mod_ref[0, 1:2, :], mod_ref[0, 0:1, :])
        hb = h.astype(BF16)
        h_scr[...] = hb
        g_ref[...] = _dot(hb, wg_ref[...])

    @pl.when(j < n16)
    def _():
        o16_ref[...] = _dot(h_scr[...], w_ref[...]).astype(BF16)

    @pl.when(j >= n16)
    def _():
        o32_ref[...] = _dot(h_scr[...], w_ref[...])


def _in_projection(x2d, mod3, mod_row, norm1, w_main, w_gates, cols16, tm=1024, tn=1024):
    t, d = x2d.shape
    n = w_main.shape[1]
    ng = w_gates.shape[1]
    tn = _tile(math.gcd(cols16, n - cols16), tn)
    n16 = cols16 // tn
    n32 = (n - cols16) // tn
    return pl.pallas_call(
        functools.partial(_inproj_kernel, n16=n16),
        grid=(t // tm, n16 + n32),
        in_specs=[pl.BlockSpec((tm, d), lambda i, j: (i, 0)),
                  pl.BlockSpec((1,) + mod3.shape[1:], lambda i, j: (mod_row(i * tm), 0, 0)),
                  pl.BlockSpec((1, d), lambda i, j: (0, 0)),
                  pl.BlockSpec((d, tn), lambda i, j: (0, j)),
                  pl.BlockSpec((d, ng), lambda i, j: (0, 0))],
        out_specs=[pl.BlockSpec((tm, tn), lambda i, j: (i, jnp.minimum(j, n16 - 1))),
                   pl.BlockSpec((tm, tn), lambda i, j: (i, jnp.maximum(j - n16, 0))),
                   pl.BlockSpec((tm, ng), lambda i, j: (i, 0))],
        out_shape=[jax.ShapeDtypeStruct((t, cols16), BF16),
                   jax.ShapeDtypeStruct((t, n - cols16), F32),
                   jax.ShapeDtypeStruct((t, ng), F32)],
        scratch_shapes=[pltpu.VMEM((tm, d), BF16)],
        compiler_params=_params("arbitrary", "arbitrary"),
        name="in_projection",
    )(x2d, mod3, norm1, w_main, w_gates)


def _split3(x):
    hi = x.astype(BF16)
    r = x - hi.astype(F32)
    mid = r.astype(BF16)
    lo = (r - mid.astype(F32)).astype(BF16)
    return hi, mid, lo


def _mlstm_kernel(*refs, seq, chunk, has_state, emit_state):
    q_ref, k_ref, v_ref, o_ref, g_ref, bg_ref, nrm_ref = refs[:7]
    pos = 7
    if has_state:
        c0_ref, n0_ref, m0_ref = refs[pos:pos + 3]
        pos += 3
    y_ref = refs[pos]
    pos += 1
    if emit_state:
        co_ref, no_ref, mo_ref = refs[pos:pos + 3]
        pos += 3
    bc_scr, a_scr, at_scr, hacc_scr = refs[pos:pos + 4]

    L = chunk
    nc = seq // L
    p = pl.program_id(1)

    @pl.when(p == 0)
    def _():
        row = lax.broadcasted_iota(jnp.int32, (L, L), 0)
        col = lax.broadcasted_iota(jnp.int32, (L, L), 1)
        tril = jnp.where(col <= row, 1.0, 0.0).astype(BF16)
        triu = jnp.where(col >= row, 1.0, 0.0).astype(BF16)
        lane = lax.broadcasted_iota(jnp.int32, (L, LANES), 1)
        for c in range(nc):
            rows = pl.ds(c * L, L)
            g = g_ref[rows, :] + bg_ref[...]
            g = GATE_CAP * jnp.tanh(g / GATE_CAP)
            gi = g[:, :LANES]
            gf = g[:, LANES:]
            lf = jnp.minimum(gf, 0.0) - jnp.log(1.0 + jnp.exp(-jnp.abs(gf)))
            hi, mid, lo = _split3(lf)
            pre = _dot(tril, hi) + _dot(tril, mid) + _dot(tril, lo)
            suf = _dot(triu, hi) + _dot(triu, mid) + _dot(triu, lo)
            bc = jnp.where(lane < H_A, pre, suf)
            a = gi - bc
            bc_scr[rows, :] = bc
            a_scr[rows, :] = a
            at_scr[c] = a.T
        if emit_state:
            mo_ref[...] = jnp.zeros(mo_ref.shape, F32)

    lane_row = lax.broadcasted_iota(jnp.int32, (1, LANES), 1)
    lane_l = lax.broadcasted_iota(jnp.int32, (L, LANES), 1)
    trow = lax.broadcasted_iota(jnp.int32, (L, L), 0)
    tcol = lax.broadcasted_iota(jnp.int32, (L, L), 1)
    m_rows = lax.broadcasted_iota(jnp.int32, (2, H_A), 0)
    m_cols = lax.broadcasted_iota(jnp.int32, (2, H_A), 1)

    n_final = [[None, None], [None, None]]
    for e in (0, 1):
        head = 2 * p + e
        hmask = (lane_row >= DQK_A) if e else (lane_row < DQK_A)
        vcols = slice(e * DV_A, (e + 1) * DV_A)
        for d in (0, 1):
            colidx = head + H_A * d
            causal = (tcol <= trow) if d == 0 else (tcol >= trow)
            sel_m = (m_rows == d) & (m_cols == head)

            if has_state:
                c0 = c0_ref[0, d, e]
                z = jnp.zeros_like(c0)
                cst0 = jnp.concatenate([z, c0] if e else [c0, z], axis=0)
                n0 = jnp.where(hmask, n0_ref[0, d, pl.ds(p, 1), :], 0.0)
                m0 = jnp.sum(jnp.sum(jnp.where(sel_m, m0_ref[0], 0.0), axis=1, keepdims=True),
                             axis=0, keepdims=True)
            else:
                cst0 = jnp.zeros((2 * DQK_A, DV_A), F32)
                n0 = jnp.zeros((1, LANES), F32)
                m0 = jnp.zeros((1, 1), F32)

            def chunk_step(ci, carry, e=e, d=d, hmask=hmask, vcols=vcols, colidx=colidx,
                           causal=causal):
                cst, nrow, m = carry
                c = ci if d == 0 else nc - 1 - ci
                r0 = c * L
                if not isinstance(r0, int):
                    r0 = pl.multiple_of(r0, L)
                rows = pl.ds(r0, L)
                qm = jnp.where(hmask, q_ref[rows, :].astype(F32) * (DQK_A ** -0.5), 0.0)
                qmb = qm.astype(BF16)
                kb = k_ref[rows, :]
                vh = v_ref[rows, vcols]
                s = _dot_nt(qmb, kb)
                pick = lane_l == colidx
                bcol = jnp.sum(jnp.where(pick, bc_scr[rows, :], 0.0), axis=1, keepdims=True)
                acol = jnp.sum(jnp.where(pick, a_scr[rows, :], 0.0), axis=1, keepdims=True)
                arow = at_scr[c, pl.ds(colidx, 1), :]
                logd = jnp.where(causal, bcol + arow, NEG_BIG)
                m_inter = bcol + m
                m_t = jnp.maximum(m_inter, jnp.max(logd, axis=1, keepdims=True))
                pmat = s * jnp.exp(logd - m_t)
                inter = jnp.exp(m_inter - m_t)
                num = _dot(pmat.astype(BF16), vh) + inter * _dot(qmb, cst.astype(BF16))
                den = (jnp.sum(pmat, axis=1, keepdims=True)
                       + inter * jnp.sum(qm * nrow, axis=1, keepdims=True))
                hh = num / jnp.maximum(jnp.abs(den), jnp.exp(-m_t))
                if d == 0:
                    hacc_scr[e, rows, :] = hh
                else:
                    ht = hacc_scr[e, rows, :] + hh
                    ms = jnp.mean(ht * ht, axis=1, keepdims=True)
                    hn = ht * lax.rsqrt(ms + EPS) * nrm_ref[:, vcols]
                    og = o_ref[rows, vcols].astype(F32)
                    y_ref[rows, vcols] = (hn * _sigmoid(og)).astype(BF16)
                b_last = bcol[L - 1:L, :] if d == 0 else bcol[0:1, :]
                logw = b_last + acol
                m_new = jnp.maximum(b_last + m, jnp.max(logw, axis=0, keepdims=True))
                wk = jnp.exp(logw - m_new) * jnp.where(hmask, kb.astype(F32), 0.0)
                decay = jnp.exp(b_last + m - m_new)
                cst = decay * cst + _dot_tn(wk.astype(BF16), vh)
                nrow = decay * nrow + jnp.sum(wk, axis=0, keepdims=True)
                return cst, nrow, m_new

            if nc == 1:
                cst, nrow, m = chunk_step(0, (cst0, n0, m0))
            else:
                cst, nrow, m = lax.fori_loop(0, nc, chunk_step, (cst0, n0, m0))

            if emit_state:
                co_ref[0, d, e] = cst[e * DQK_A:(e + 1) * DQK_A, :]
                mo_ref[0] = jnp.where(sel_m, m, mo_ref[0])
                n_final[e][d] = nrow

    if emit_state:
        for d in (0, 1):
            no_ref[0, d, pl.ds(p, 1), :] = n_final[0][d] + n_final[1][d]


def _mlstm(proj, col0, gates, b_gates2, mlstm_norm, batch, seq, chunk, state=None,
           emit_state=False):
    npairs = H_A // 2
    qblk = col0 // LANES
    kblk = qblk + (H_A * DQK_A) // LANES
    vblk = (col0 + 2 * H_A * DQK_A) // (2 * DV_A)
    oblk = vblk + (H_A * DV_A) // (2 * DV_A)
    ng = gates.shape[1]
    in_specs = [pl.BlockSpec((seq, LANES), lambda b, p: (b, qblk + p)),
                pl.BlockSpec((seq, LANES), lambda b, p: (b, kblk + p)),
                pl.BlockSpec((seq, 2 * DV_A), lambda b, p: (b, vblk + p)),
                pl.BlockSpec((seq, 2 * DV_A), lambda b, p: (b, oblk + p)),
                pl.BlockSpec((seq, ng), lambda b, p: (b, 0)),
                pl.BlockSpec((1, ng), lambda b, p: (0, 0)),
                pl.BlockSpec((1, 2 * DV_A), lambda b, p: (0, p))]
    args = [proj, proj, proj, proj, gates, b_gates2, mlstm_norm]
    if state is not None:
        c0, n0, m0 = state
        in_specs += [pl.BlockSpec((1, 2, 2, DQK_A, DV_A), lambda b, p: (b, 0, p, 0, 0)),
                     pl.BlockSpec((1, 2, npairs, LANES), lambda b, p: (b, 0, 0, 0)),
                     pl.BlockSpec((1, 2, H_A), lambda b, p: (b, 0, 0))]
        args += [c0, n0, m0]
    out_specs = [pl.BlockSpec((seq, 2 * DV_A), lambda b, p: (b, p))]
    out_shape = [jax.ShapeDtypeStruct((batch * seq, H_A * DV_A), BF16)]
    if emit_state:
        out_specs += [pl.BlockSpec((1, 2, 2, DQK_A, DV_A), lambda b, p: (b, 0, p, 0, 0)),
                      pl.BlockSpec((1, 2, npairs, LANES), lambda b, p: (b, 0, 0, 0)),
                      pl.BlockSpec((1, 2, H_A), lambda b, p: (b, 0, 0))]
        out_shape += [jax.ShapeDtypeStruct((batch, 2, H_A, DQK_A, DV_A), F32),
                      jax.ShapeDtypeStruct((batch, 2, npairs, LANES), F32),
                      jax.ShapeDtypeStruct((batch, 2, H_A), F32)]
    return pl.pallas_call(
        functools.partial(_mlstm_kernel, seq=seq, chunk=chunk, has_state=state is not None,
                          emit_state=emit_state),
        grid=(batch, npairs),
        in_specs=in_specs,
        out_specs=out_specs,
        out_shape=out_shape,
        scratch_shapes=[pltpu.VMEM((seq, LANES), F32), pltpu.VMEM((seq, LANES), F32),
                        pltpu.VMEM((seq // chunk, LANES, chunk), F32),
                        pltpu.VMEM((2, seq, DV_A), F32)],
        compiler_params=_params("arbitrary", "arbitrary"),
        name="mlstm",
    )(*args)


def _rope(x, cos, sin_signed, lane):
    ahead = pltpu.roll(x, LANES - DH_B // 4, axis=1)
    behind = pltpu.roll(x, DH_B // 4, axis=1)
    even_quarter = (lane // (DH_B // 4)) % 2 == 0
    return x * cos + jnp.where(even_quarter, ahead, behind) * sin_signed


def _ones_column(rows):
    lane = lax.broadcasted_iota(jnp.int32, (rows, LANES), 1)
    return jnp.where(lane == 0, 1.0, 0.0).astype(BF16)


def _lambda(lam_ref, lambda_init):
    lp = lam_ref[...]
    return (jnp.exp(jnp.sum(lp[0:1] * lp[1:2], axis=1, keepdims=True))
            - jnp.exp(jnp.sum(lp[2:3] * lp[3:4], axis=1, keepdims=True)) + lambda_init)


def _diff_attend(q, keys, v_aug, lam, dn, lambda_init):
    lane = lax.broadcasted_iota(jnp.int32, (1, LANES), 1)
    q = q * (DH_B ** -0.5 * LOG2_E)

    def softmax_pv(s):
        ex = jnp.exp2(s - jnp.max(s, axis=1, keepdims=True)).astype(BF16)
        r = _dot(ex, v_aug)
        return r[:, :DV_B] * (1.0 / r[:, DV_B:DV_B + 1])

    s1 = _dot_nt(jnp.where(lane < DH_B, q, 0.0).astype(BF16), keys)
    s2 = _dot_nt(jnp.where(lane >= DH_B, q, 0.0).astype(BF16), keys)
    o = softmax_pv(s1) - lam * softmax_pv(s2)
    ms = jnp.mean(o * o, axis=1, keepdims=True)
    return (o * lax.rsqrt(ms + EPS) * dn) * (1.0 - lambda_init)


def _attn_latent_kernel(q_ref, k_ref, v_ref, ck_ref, cv_ref, cos_ref, sin_ref, lam_ref, dn_ref,
                        o_ref, k_scr, v_scr, *, tq, sub, seq, lambda_init):
    qi = pl.program_id(2)
    lane = lax.broadcasted_iota(jnp.int32, (1, LANES), 1)

    @pl.when(qi == 0)
    def _():
        k_scr[0:seq, :] = _rope(k_ref[...], cos_ref[...], sin_ref[...], lane).astype(BF16)
        k_scr[seq:, :] = ck_ref[0, 0].astype(BF16)
        v_scr[0:seq, 0:DV_B] = v_ref[...].astype(BF16)
        v_scr[seq:, 0:DV_B] = cv_ref[0, 0].astype(BF16)
        v_scr[:, DV_B:] = _ones_column(v_scr.shape[0])

    lam = _lambda(lam_ref, lambda_init)
    for r0 in range(0, tq, sub):
        rows = pl.ds(pl.multiple_of(qi * tq + r0, sub), sub)
        q = _rope(q_ref[r0:r0 + sub, :], cos_ref[rows, :], sin_ref[rows, :], lane)
        o = _diff_attend(q, k_scr[...], v_scr[...], lam, dn_ref[...], lambda_init)
        o_ref[r0:r0 + sub, :] = o.astype(BF16)


def _attn_context_kernel(q_ref, k_ref, v_ref, lam_ref, dn_ref, o_ref, nk_ref, nv_ref, *,
                         lambda_init):
    lam = _lambda(lam_ref, lambda_init)
    ones = _ones_column(q_ref.shape[0])
    for h in range(H_B):
        cols = slice(h * LANES, (h + 1) * LANES)
        k = k_ref[:, cols]
        v = v_ref[:, cols]
        nk_ref[0, h] = k
        nv_ref[0, h] = v
        v_aug = jnp.concatenate([v.astype(BF16), ones], axis=1)
        o = _diff_attend(q_ref[:, cols], k.astype(BF16), v_aug, lam, dn_ref[...], lambda_init)
        o_ref[:, cols] = o.astype(BF16)


def _attention_context(qkv, lam_params, diff_norm, batch, seq, lambda_init):
    w = H_B * LANES
    return pl.pallas_call(
        functools.partial(_attn_context_kernel, lambda_init=lambda_init),
        grid=(batch,),
        in_specs=[pl.BlockSpec((seq, w), lambda b: (b, 0)),
                  pl.BlockSpec((seq, w), lambda b: (b, 1)),
                  pl.BlockSpec((seq, w), lambda b: (b, 2)),
                  pl.BlockSpec(lam_params.shape, lambda b: (0, 0)),
                  pl.BlockSpec((1, DV_B), lambda b: (0, 0))],
        out_specs=[pl.BlockSpec((seq, w), lambda b: (b, 0)),
                   pl.BlockSpec((1, H_B, seq, LANES), lambda b: (b, 0, 0, 0)),
                   pl.BlockSpec((1, H_B, seq, DV_B), lambda b: (b, 0, 0, 0))],
        out_shape=[jax.ShapeDtypeStruct((batch * seq, w), BF16),
                   jax.ShapeDtypeStruct((batch, H_B, seq, LANES), F32),
                   jax.ShapeDtypeStruct((batch, H_B, seq, DV_B), F32)],
        compiler_params=_params("arbitrary"),
        name="attention_context",
    )(qkv, qkv, qkv, lam_params, diff_norm)


def _attention_latent(qkv, lam_params, diff_norm, batch, seq, tq, lambda_init, cache, rope_tables):
    ck, cv = cache
    past = ck.shape[2]
    cos, sin_signed = rope_tables
    nq = seq // tq
    return pl.pallas_call(
        functools.partial(_attn_latent_kernel, tq=tq, sub=min(256, tq), seq=seq,
                          lambda_init=lambda_init),
        grid=(batch, H_B, nq),
        in_specs=[pl.BlockSpec((tq, LANES), lambda b, h, i: (b * nq + i, h)),
                  pl.BlockSpec((seq, LANES), lambda b, h, i: (b, H_B + h)),
                  pl.BlockSpec((seq, LANES), lambda b, h, i: (b, 2 * H_B + h)),
                  pl.BlockSpec((1, 1, past, LANES), lambda b, h, i: (b, h, 0, 0)),
                  pl.BlockSpec((1, 1, past, LANES), lambda b, h, i: (b, h, 0, 0)),
                  pl.BlockSpec((seq, LANES), lambda b, h, i: (0, 0)),
                  pl.BlockSpec((seq, LANES), lambda b, h, i: (0, 0)),
                  pl.BlockSpec(lam_params.shape, lambda b, h, i: (0, 0)),
                  pl.BlockSpec((1, DV_B), lambda b, h, i: (0, 0))],
        out_specs=pl.BlockSpec((tq, DV_B), lambda b, h, i: (b * nq + i, h)),
        out_shape=jax.ShapeDtypeStruct((batch * seq, H_B * DV_B), BF16),
        scratch_shapes=[pltpu.VMEM((seq + past, LANES), BF16),
                        pltpu.VMEM((seq + past, 2 * DV_B), BF16)],
        compiler_params=_params("arbitrary", "arbitrary", "arbitrary"),
        name="attention_latent",
    )(qkv, qkv, qkv, ck, cv, cos, sin_signed, lam_params, diff_norm)


def _merge_kernel(ya_ref, yb_ref, ga_ref, gb_ref, x_ref, mod_ref, wpa_ref, wpb_ref, wo_ref, o_ref):
    ya = _dot(ya_ref[...], wpa_ref[...])
    yb = _dot(yb_ref[...], wpb_ref[...])
    y = _sigmoid(ga_ref[...].astype(F32)) * ya + _sigmoid(gb_ref[...].astype(F32)) * yb
    z = _dot(y.astype(BF16), wo_ref[...])
    o_ref[...] = x_ref[...] + mod_ref[0, 2:3, :] * z


def _merge_project(y_a, y_b, proj, x2d, mod3, mod_row, w_pa, w_pb, w_out, tm=512):
    t, d = x2d.shape
    wa = y_a.shape[1]
    wb = y_b.shape[1]
    const = dict(pipeline_mode=pl.Buffered(1))
    return pl.pallas_call(
        _merge_kernel,
        grid=(t // tm,),
        in_specs=[pl.BlockSpec((tm, wa), lambda i: (i, 0)),
                  pl.BlockSpec((tm, wb), lambda i: (i, 0)),
                  pl.BlockSpec((tm, d), lambda i: (i, 0)),
                  pl.BlockSpec((tm, d), lambda i: (i, 1)),
                  pl.BlockSpec((tm, d), lambda i: (i, 0)),
                  pl.BlockSpec((1,) + mod3.shape[1:], lambda i: (mod_row(i * tm), 0, 0)),
                  pl.BlockSpec((wa, d), lambda i: (0, 0), **const),
                  pl.BlockSpec((wb, d), lambda i: (0, 0), **const),
                  pl.BlockSpec((d, d), lambda i: (0, 0), **const)],
        out_specs=pl.BlockSpec((tm, d), lambda i: (i, 0)),
        out_shape=jax.ShapeDtypeStruct((t, d), F32),
        compiler_params=_params("arbitrary"),
        name="merge_project",
    )(y_a, y_b, proj, proj, x2d, mod3, w_pa, w_pb, w_out)


def _ffn_kernel(x_ref, xp_ref, xn_ref, mod_ref, n2_ref, wa_ref, wb_ref, cwa_ref, cwb_ref,
                cba_ref, cbb_ref, wd_ref, fn_ref, o_ref, h_scr, acc_scr, ua_scr, ub_scr,
                *, tm, seq):
    i = pl.program_id(0)
    j = pl.program_id(1)
    halo = BF16_ROWS

    @pl.when(j == 0)
    def _():
        def nrm(x):
            return _modulated_norm(x, n2_ref[...], mod_ref[0, 4:5, :], mod_ref[0, 3:4, :])
        h_scr[0:halo, :] = nrm(xp_ref[...]).astype(BF16)
        h_scr[halo:halo + tm, :] = nrm(x_ref[...]).astype(BF16)
        h_scr[halo + tm:, :] = nrm(xn_ref[...]).astype(BF16)
        acc_scr[...] = jnp.zeros(acc_scr.shape, F32)

    h = h_scr[...]
    ua_scr[...] = _dot(h, wa_ref[...])
    ub_scr[...] = _dot(h, wb_ref[...])

    pos = (i * tm + lax.broadcasted_iota(jnp.int32, (tm, 1), 0)) % seq
    has_prev = jnp.where(pos != 0, 1.0, 0.0)
    has_next = jnp.where(pos != seq - 1, 1.0, 0.0)

    def conv(u_scr, cw_ref, cb_ref):
        prev = u_scr[halo - 1:halo - 1 + tm, :] * has_prev
        cur = u_scr[halo:halo + tm, :]
        nxt = u_scr[halo + 1:halo + 1 + tm, :] * has_next
        return prev * cw_ref[0:1, :] + cur * cw_ref[1:2, :] + nxt * cw_ref[2:3, :] + cb_ref[...]

    a = conv(ua_scr, cwa_ref, cba_ref)
    b = conv(ub_scr, cwb_ref, cbb_ref)
    g = (a * _sigmoid(a)) * b
    acc_scr[...] += _dot(g.astype(BF16), wd_ref[...])

    @pl.when(j == pl.num_programs(1) - 1)
    def _():
        x2 = x_ref[...] + mod_ref[0, 5:6, :] * acc_scr[...]
        ms = jnp.mean(x2 * x2, axis=-1, keepdims=True)
        o_ref[...] = x2 * lax.rsqrt(ms + EPS) * fn_ref[...]


def _conv_ffn(x2d, mod3, mod_row, norm2, w_up, conv_w, conv_b, w_down, final_norm, seq,
              tm=512, tf=512):
    t, d = x2d.shape
    dff = w_down.shape[0]
    nf = dff // tf
    halo = BF16_ROWS
    nhalo = t // halo
    per = tm // halo
    return pl.pallas_call(
        functools.partial(_ffn_kernel, tm=tm, seq=seq),
        grid=(t // tm, nf),
        in_specs=[pl.BlockSpec((tm, d), lambda i, j: (i, 0)),
                  pl.BlockSpec((halo, d), lambda i, j: (jnp.maximum(i * per - 1, 0), 0)),
                  pl.BlockSpec((halo, d), lambda i, j: (jnp.minimum((i + 1) * per, nhalo - 1), 0)),
                  pl.BlockSpec((1,) + mod3.shape[1:], lambda i, j: (mod_row(i * tm), 0, 0)),
                  pl.BlockSpec((1, d), lambda i, j: (0, 0)),
                  pl.BlockSpec((d, tf), lambda i, j: (0, j)),
                  pl.BlockSpec((d, tf), lambda i, j: (0, nf + j)),
                  pl.BlockSpec((3, tf), lambda i, j: (0, j)),
                  pl.BlockSpec((3, tf), lambda i, j: (0, nf + j)),
                  pl.BlockSpec((1, tf), lambda i, j: (0, j)),
                  pl.BlockSpec((1, tf), lambda i, j: (0, nf + j)),
                  pl.BlockSpec((tf, d), lambda i, j: (j, 0)),
                  pl.BlockSpec((1, d), lambda i, j: (0, 0))],
        out_specs=pl.BlockSpec((tm, d), lambda i, j: (i, 0)),
        out_shape=jax.ShapeDtypeStruct((t, d), F32),
        scratch_shapes=[pltpu.VMEM((tm + 2 * halo, d), BF16), pltpu.VMEM((tm, d), F32),
                        pltpu.VMEM((tm + 2 * halo, tf), F32), pltpu.VMEM((tm + 2 * halo, tf), F32)],
        compiler_params=_params("arbitrary", "arbitrary"),
        name="conv_ffn",
    )(x2d, x2d, x2d, mod3, norm2, w_up, w_up, conv_w, conv_w, conv_b, conv_b, w_down, final_norm)


def _rope_tables(n_tokens):
    rows = n_tokens // GRID_W
    row = jnp.repeat(jnp.arange(rows), GRID_W).astype(F32)
    col = jnp.tile(jnp.arange(GRID_W), rows).astype(F32)
    quarter = DH_B // 4
    inv_freq = jnp.power(ROPE_BASE, -jnp.arange(quarter, dtype=F32) / quarter)
    ang_r = row[:, None] * inv_freq
    ang_c = col[:, None] * inv_freq
    ang = jnp.concatenate([ang_r, ang_r, ang_c, ang_c] * 2, axis=-1)
    sign = jnp.where((jnp.arange(2 * DH_B) // quarter) % 2 == 0, -1.0, 1.0).astype(F32)
    return jnp.cos(ang), jnp.sin(ang) * sign


def _gate_columns(g):
    lead = g.shape[:-1]
    g4 = g.reshape(lead + (4, H_A))
    pad = jnp.zeros(lead + (LANES - 2 * H_A,), g.dtype)
    return jnp.concatenate([g4[..., 0, :], g4[..., 2, :], pad, g4[..., 1, :], g4[..., 3, :], pad],
                           axis=-1)


def _trunk(x2d, batch, seq, mod3, mod_row, wts, lambda_init, chunk, cache=None, state=None,
           rope_tables=None, emit_state=False):
    t, d = x2d.shape
    span = t if cache is None else seq
    proj, qkv, gates = _in_projection(x2d, mod3, mod_row, wts["norm1"], wts["w_main"],
                                      wts["w_gates"], cols16=wts["cols16"], tm=min(1024, span))
    a_out = _mlstm(proj, 2 * d, gates, wts["b_gates"], wts["mlstm_norm"], batch, seq, chunk,
                   state=state, emit_state=emit_state)
    if cache is None:
        b_out = _attention_context(qkv, wts["lam"], wts["diff_norm"], batch, seq, lambda_init)
    else:
        b_out = [_attention_latent(qkv, wts["lam"], wts["diff_norm"], batch, seq, 512, lambda_init,
                                   cache, rope_tables)]
    x1 = _merge_project(a_out[0], b_out[0], proj, x2d, mod3, mod_row, wts["w_pa"], wts["w_pb"],
                        wts["w_out"], tm=min(512, span))
    y = _conv_ffn(x1, mod3, mod_row, wts["norm2"], wts["w_up"], wts["conv_w"], wts["conv_b"],
                  wts["w_down"], wts["final_norm"], seq, tm=min(512, span))
    return y, a_out[1:], b_out[1:]


def kernel(x_prompt, x_sample, cache_k, cache_v, state_C, state_n, state_m, c, c_ctx, w_mod, b_mod, norm1, w_in, b_gates, mlstm_norm, lam_q1, lam_k1, lam_q2, lam_k2, diff_norm, w_pa, w_pb, w_out, norm2, w_up, conv_w, conv_b, w_down, final_norm):
    assert w_in.shape[0] == 1, "single trunk layer"
    bp, sp, d = x_prompt.shape
    bs, ss, _ = x_sample.shape
    past = cache_k.shape[3]
    lambda_init = 0.8 - 0.6 * math.exp(-0.3 * 0)

    wa_cols = 2 * H_A * DQK_A + 2 * H_A * DV_A
    wb_cols = 3 * H_B * 2 * DH_B
    g0 = wa_cols
    b0 = g0 + 4 * H_A
    m0 = b0 + wb_cols
    w = w_in.reshape(w_in.shape[1:])
    wts = {
        "w_main": jnp.concatenate([w[:, m0:].astype(BF16), w[:, :g0].astype(BF16),
                                   w[:, b0:m0].astype(BF16)], axis=1),
        "cols16": 2 * d + wa_cols,
        "w_gates": _gate_columns(w[:, g0:b0]).astype(BF16),
        "b_gates": _gate_columns(b_gates.reshape(-1))[None, :],
        "norm1": norm1, "norm2": norm2, "mlstm_norm": mlstm_norm, "diff_norm": diff_norm,
        "lam": jnp.concatenate([lam_q1, lam_k1, lam_q2, lam_k2], axis=0),
        "w_pa": w_pa.reshape(w_pa.shape[1:]).astype(BF16),
        "w_pb": w_pb.reshape(w_pb.shape[1:]).astype(BF16),
        "w_out": w_out.reshape(w_out.shape[1:]).astype(BF16),
        "w_up": w_up.reshape(w_up.shape[1:]).astype(BF16),
        "conv_w": conv_w.reshape(conv_w.shape[1:]), "conv_b": conv_b,
        "w_down": w_down.reshape(w_down.shape[1:]).astype(BF16),
        "final_norm": final_norm[None, :],
    }

    c_all = jnp.concatenate([c_ctx[None, :], c, jnp.zeros((8 - 1 - bs, d), F32)], axis=0)
    mod3 = _modulation(c_all, w_mod.reshape(w_mod.shape[1:]), b_mod).reshape(8, 6, d)

    yp, (c_new, n_new, m_new), (k_new, v_new) = _trunk(
        x_prompt.reshape(bp * sp, d), bp, sp, mod3, lambda t: 0, wts, lambda_init,
        chunk=sp, emit_state=True)

    state = (state_C.reshape(bs, 2, H_A, DQK_A, DV_A), state_n.reshape(bs, 2, H_A // 2, LANES),
             state_m.reshape(bs, 2, H_A))
    cache = (cache_k.reshape(bs, H_B, past, 2 * DH_B), cache_v.reshape(bs, H_B, past, DV_B))
    ys, _, _ = _trunk(
        x_sample.reshape(bs * ss, d), bs, ss, mod3, lambda t: 1 + t // ss, wts, lambda_init,
        chunk=256, cache=cache, state=state, rope_tables=_rope_tables(ss))

    return (yp.reshape(bp, sp, d), ys.reshape(bs, ss, d),
            k_new.reshape(bp, 1, H_B, sp, 2 * DH_B), v_new.reshape(bp, 1, H_B, sp, DV_B),
            c_new.reshape(bp, 1, 2, H_A, DQK_A, DV_A), n_new.reshape(bp, 1, 2, H_A, DQK_A),
            m_new.reshape(bp, 1, 2, H_A))
```

```python
import functools
import math

import jax
import jax.numpy as jnp
from jax import lax
from jax.experimental import pallas as pl
from jax.experimental.pallas import tpu as pltpu

F32 = jnp.float32
BF16 = jnp.bfloat16

GRID_W = 64
H_A = 8
DQK_A = 64
DV_A = 128
H_B = 8
DH_B = 64
DV_B = 128
ROPE_BASE = 10000.0
GATE_CAP = 15.0
EPS = 1e-6
NEG_BIG = -1e30
LOG2_E = 1.4426950408889634

LANES = 128
BF16_ROWS = 16
MXU_COLS = 256
VMEM_LIMIT = 56 * 1024 * 1024


def _sigmoid(x):
    return 1.0 / (1.0 + jnp.exp(-x))


def _dot(a, b):
    return jnp.dot(a, b, preferred_element_type=F32)


def _dot_nt(a, b):
    return lax.dot_general(a, b, (((1,), (1,)), ((), ())), preferred_element_type=F32)


def _dot_tn(a, b):
    return lax.dot_general(a, b, (((0,), (0,)), ((), ())), preferred_element_type=F32)


def _tile(n, want):
    t = min(want, n)
    while n % t:
        t -= LANES
    return t


def _params(*sem):
    return pltpu.CompilerParams(dimension_semantics=sem, vmem_limit_bytes=VMEM_LIMIT)


def _mod_kernel(c_ref, w_ref, b_ref, o_ref):
    c = c_ref[...]
    s = (c * _sigmoid(c)).astype(BF16)
    o_ref[...] = _dot(s, w_ref[...].astype(BF16)) + b_ref[...]


def _modulation(c_all, w_mod, b_mod, tn=1024):
    rows, d = c_all.shape
    n = w_mod.shape[1]
    tn = _tile(n, tn)
    return pl.pallas_call(
        _mod_kernel,
        grid=(n // tn,),
        in_specs=[pl.BlockSpec((rows, d), lambda j: (0, 0)),
                  pl.BlockSpec((d, tn), lambda j: (0, j)),
                  pl.BlockSpec((1, tn), lambda j: (0, j))],
        out_specs=pl.BlockSpec((rows, tn), lambda j: (0, j)),
        out_shape=jax.ShapeDtypeStruct((rows, n), F32),
        compiler_params=_params("arbitrary"),
        name="modulation",
    )(c_all, w_mod, b_mod)


def _modulated_norm(x, w, scale, shift):
    ms = jnp.mean(x * x, axis=-1, keepdims=True)
    return (x * lax.rsqrt(ms + EPS) * w) * (1.0 + scale) + shift


def _inproj_kernel(x_ref, mod_ref, n1_ref, w_ref, wg_ref, o16_ref, o32_ref, g_ref, h_scr, *, n16):
    j = pl.program_id(1)

    @pl.when(j == 0)
    def _():
        h = _modulated_norm(x_ref[...], n1_ref[...], mod_ref[0, 1:2, :], mod_ref[0, 0:1, :])
        hb = h.astype(BF16)
        h_scr[...] = hb
        g_ref[...] = _dot(hb, wg_ref[...])

    @pl.when(j < n16)
    def _():
        o16_ref[...] = _dot(h_scr[...], w_ref[...]).astype(BF16)

    @pl.when(j >= n16)
    def _():
        o32_ref[...] = _dot(h_scr[...], w_ref[...])


def _in_projection(x2d, mod3, mod_row, norm1, w_main, w_gates, cols16, tm=1024, tn=1024):
    t, d = x2d.shape
    n = w_main.shape[1]
    ng = w_gates.shape[1]
    tn = _tile(math.gcd(cols16, n - cols16), tn)
    n16 = cols16 // tn
    n32 = (n - cols16) // tn
    return pl.pallas_call(
        functools.partial(_inproj_kernel, n16=n16),
        grid=(t // tm, n16 + n32),
        in_specs=[pl.BlockSpec((tm, d), lambda i, j: (i, 0)),
                  pl.BlockSpec((1,) + mod3.shape[1:], lambda i, j: (mod_row(i * tm), 0, 0)),
                  pl.BlockSpec((1, d), lambda i, j: (0, 0)),
                  pl.BlockSpec((d, tn), lambda i, j: (0, j)),
                  pl.BlockSpec((d, ng), lambda i, j: (0, 0))],
        out_specs=[pl.BlockSpec((tm, tn), lambda i, j: (i, jnp.minimum(j, n16 - 1))),
                   pl.BlockSpec((tm, tn), lambda i, j: (i, jnp.maximum(j - n16, 0))),
                   pl.BlockSpec((tm, ng), lambda i, j: (i, 0))],
        out_shape=[jax.ShapeDtypeStruct((t, cols16), BF16),
                   jax.ShapeDtypeStruct((t, n - cols16), F32),
                   jax.ShapeDtypeStruct((t, ng), F32)],
        scratch_shapes=[pltpu.VMEM((tm, d), BF16)],
        compiler_params=_params("arbitrary", "arbitrary"),
        name="in_projection",
    )(x2d, mod3, norm1, w_main, w_gates)


def _split3(x):
    hi = x.astype(BF16)
    r = x - hi.astype(F32)
    mid = r.astype(BF16)
    lo = (r - mid.astype(F32)).astype(BF16)
    return hi, mid, lo


def _mlstm_kernel(*refs, seq, chunk, has_state, emit_state):
    q_ref, k_ref, v_ref, o_ref, g_ref, bg_ref, nrm_ref = refs[:7]
    pos = 7
    if has_state:
        c0_ref, n0_ref, m0_ref = refs[pos:pos + 3]
        pos += 3
    y_ref = refs[pos]
    pos += 1
    if emit_state:
        co_ref, no_ref, mo_ref = refs[pos:pos + 3]
        pos += 3
    bc_scr, a_scr, at_scr, hacc_scr = refs[pos:pos + 4]

    L = chunk
    nc = seq // L
    p = pl.program_id(1)

    @pl.when(p == 0)
    def _():
        row = lax.broadcasted_iota(jnp.int32, (L, L), 0)
        col = lax.broadcasted_iota(jnp.int32, (L, L), 1)
        tril = jnp.where(col <= row, 1.0, 0.0).astype(BF16)
        triu = jnp.where(col >= row, 1.0, 0.0).astype(BF16)
        lane = lax.broadcasted_iota(jnp.int32, (L, LANES), 1)
        for c in range(nc):
            rows = pl.ds(c * L, L)
            g = g_ref[rows, :] + bg_ref[...]
            g = GATE_CAP * jnp.tanh(g / GATE_CAP)
            gi = g[:, :LANES]
            gf = g[:, LANES:]
            lf = jnp.minimum(gf, 0.0) - jnp.log(1.0 + jnp.exp(-jnp.abs(gf)))
            hi, mid, lo = _split3(lf)
            pre = _dot(tril, hi) + _dot(tril, mid) + _dot(tril, lo)
            suf = _dot(triu, hi) + _dot(triu, mid) + _dot(triu, lo)
            bc = jnp.where(lane < H_A, pre, suf)
            a = gi - bc
            bc_scr[rows, :] = bc
            a_scr[rows, :] = a
            at_scr[c] = a.T
        if emit_state:
            mo_ref[...] = jnp.zeros(mo_ref.shape, F32)

    lane_row = lax.broadcasted_iota(jnp.int32, (1, LANES), 1)
    lane_l = lax.broadcasted_iota(jnp.int32, (L, LANES), 1)
    trow = lax.broadcasted_iota(jnp.int32, (L, L), 0)
    tcol = lax.broadcasted_iota(jnp.int32, (L, L), 1)
    m_rows = lax.broadcasted_iota(jnp.int32, (2, H_A), 0)
    m_cols = lax.broadcasted_iota(jnp.int32, (2, H_A), 1)

    chains = [(e, d) for e in (0, 1) for d in (0, 1)]

    def head_mask(e):
        return (lane_row >= DQK_A) if e else (lane_row < DQK_A)

    def initial_state(e, d):
        if not has_state:
            return (jnp.zeros((2 * DQK_A, DV_A), F32), jnp.zeros((1, LANES), F32),
                    jnp.zeros((1, 1), F32))
        c0 = c0_ref[0, d, e]
        z = jnp.zeros_like(c0)
        sel_m = (m_rows == d) & (m_cols == 2 * p + e)
        m0 = jnp.sum(jnp.sum(jnp.where(sel_m, m0_ref[0], 0.0), axis=1, keepdims=True),
                     axis=0, keepdims=True)
        return (jnp.concatenate([z, c0] if e else [c0, z], axis=0),
                jnp.where(head_mask(e), n0_ref[0, d, pl.ds(p, 1), :], 0.0), m0)

    def chunk_rows(c):
        r0 = c * L
        return pl.ds(r0 if isinstance(r0, int) else pl.multiple_of(r0, L), L)

    def chunk_step(ci, carry, e, d):
        cst, nrow, m = carry
        c = ci if d == 0 else nc - 1 - ci
        rows = chunk_rows(c)
        hmask = head_mask(e)
        vcols = slice(e * DV_A, (e + 1) * DV_A)
        colidx = 2 * p + e + H_A * d
        causal = (tcol <= trow) if d == 0 else (tcol >= trow)
        qm = jnp.where(hmask, q_ref[rows, :].astype(F32) * (DQK_A ** -0.5), 0.0)
        qmb = qm.astype(BF16)
        kb = k_ref[rows, :]
        vh = v_ref[rows, vcols]
        s = _dot_nt(qmb, kb)
        pick = lane_l == colidx
        bcol = jnp.sum(jnp.where(pick, bc_scr[rows, :], 0.0), axis=1, keepdims=True)
        acol = jnp.sum(jnp.where(pick, a_scr[rows, :], 0.0), axis=1, keepdims=True)
        arow = at_scr[c, pl.ds(colidx, 1), :]
        logd = jnp.where(causal, bcol + arow, NEG_BIG)
        m_inter = bcol + m
        m_t = jnp.maximum(m_inter, jnp.max(logd, axis=1, keepdims=True))
        pmat = s * jnp.exp(logd - m_t)
        inter = jnp.exp(m_inter - m_t)
        num = _dot(pmat.astype(BF16), vh) + inter * _dot(qmb, cst.astype(BF16))
        den = (jnp.sum(pmat, axis=1, keepdims=True)
               + inter * jnp.sum(qm * nrow, axis=1, keepdims=True))
        hacc_scr[d, e, rows, :] = num / jnp.maximum(jnp.abs(den), jnp.exp(-m_t))
        b_last = bcol[L - 1:L, :] if d == 0 else bcol[0:1, :]
        logw = b_last + acol
        m_new = jnp.maximum(b_last + m, jnp.max(logw, axis=0, keepdims=True))
        wk = jnp.exp(logw - m_new) * jnp.where(hmask, kb.astype(F32), 0.0)
        decay = jnp.exp(b_last + m - m_new)
        cst = decay * cst + _dot_tn(wk.astype(BF16), vh)
        nrow = decay * nrow + jnp.sum(wk, axis=0, keepdims=True)
        return cst, nrow, m_new

    def all_chains(ci, carry):
        return tuple(chunk_step(ci, st, e, d) for (e, d), st in zip(chains, carry))

    def finish(c, _):
        rows = chunk_rows(c)
        for e in (0, 1):
            vcols = slice(e * DV_A, (e + 1) * DV_A)
            ht = hacc_scr[0, e, rows, :] + hacc_scr[1, e, rows, :]
            ms = jnp.mean(ht * ht, axis=1, keepdims=True)
            hn = ht * lax.rsqrt(ms + EPS) * nrm_ref[:, vcols]
            og = o_ref[rows, vcols].astype(F32)
            y_ref[rows, vcols] = (hn * _sigmoid(og)).astype(BF16)
        return 0

    init = tuple(initial_state(e, d) for e, d in chains)
    if nc == 1:
        final = all_chains(0, init)
        finish(0, 0)
    else:
        final = lax.fori_loop(0, nc, all_chains, init)
        lax.fori_loop(0, nc, finish, 0)

    if emit_state:
        state = dict(zip(chains, final))
        for (e, d), (cst, _, m) in state.items():
            co_ref[0, d, e] = cst[e * DQK_A:(e + 1) * DQK_A, :]
            sel_m = (m_rows == d) & (m_cols == 2 * p + e)
            mo_ref[0] = jnp.where(sel_m, m, mo_ref[0])
        for d in (0, 1):
            no_ref[0, d, pl.ds(p, 1), :] = state[(0, d)][1] + state[(1, d)][1]


def _mlstm(proj, col0, gates, b_gates2, mlstm_norm, batch, seq, chunk, state=None,
           emit_state=False):
    npairs = H_A // 2
    qblk = col0 // LANES
    kblk = qblk + (H_A * DQK_A) // LANES
    vblk = (col0 + 2 * H_A * DQK_A) // (2 * DV_A)
    oblk = vblk + (H_A * DV_A) // (2 * DV_A)
    ng = gates.shape[1]
    in_specs = [pl.BlockSpec((seq, LANES), lambda b, p: (b, qblk + p)),
                pl.BlockSpec((seq, LANES), lambda b, p: (b, kblk + p)),
                pl.BlockSpec((seq, 2 * DV_A), lambda b, p: (b, vblk + p)),
                pl.BlockSpec((seq, 2 * DV_A), lambda b, p: (b, oblk + p)),
                pl.BlockSpec((seq, ng), lambda b, p: (b, 0)),
                pl.BlockSpec((1, ng), lambda b, p: (0, 0)),
                pl.BlockSpec((1, 2 * DV_A), lambda b, p: (0, p))]
    args = [proj, proj, proj, proj, gates, b_gates2, mlstm_norm]
    if state is not None:
        c0, n0, m0 = state
        in_specs += [pl.BlockSpec((1, 2, 2, DQK_A, DV_A), lambda b, p: (b, 0, p, 0, 0)),
                     pl.BlockSpec((1, 2, npairs, LANES), lambda b, p: (b, 0, 0, 0)),
                     pl.BlockSpec((1, 2, H_A), lambda b, p: (b, 0, 0))]
        args += [c0, n0, m0]
    out_specs = [pl.BlockSpec((seq, 2 * DV_A), lambda b, p: (b, p))]
    out_shape = [jax.ShapeDtypeStruct((batch * seq, H_A * DV_A), BF16)]
    if emit_state:
        out_specs += [pl.BlockSpec((1, 2, 2, DQK_A, DV_A), lambda b, p: (b, 0, p, 0, 0)),
                      pl.BlockSpec((1, 2, npairs, LANES), lambda b, p: (b, 0, 0, 0)),
                      pl.BlockSpec((1, 2, H_A), lambda b, p: (b, 0, 0))]
        out_shape += [jax.ShapeDtypeStruct((batch, 2, H_A, DQK_A, DV_A), F32),
                      jax.ShapeDtypeStruct((batch, 2, npairs, LANES), F32),
                      jax.ShapeDtypeStruct((batch, 2, H_A), F32)]
    return pl.pallas_call(
        functools.partial(_mlstm_kernel, seq=seq, chunk=chunk, has_state=state is not None,
                          emit_state=emit_state),
        grid=(batch, npairs),
        in_specs=in_specs,
        out_specs=out_specs,
        out_shape=out_shape,
        scratch_shapes=[pltpu.VMEM((seq, LANES), F32), pltpu.VMEM((seq, LANES), F32),
                        pltpu.VMEM((seq // chunk, LANES, chunk), F32),
                        pltpu.VMEM((2, 2, seq, DV_A), F32)],
        compiler_params=_params("arbitrary", "arbitrary"),
        name="mlstm",
    )(*args)


def _rope(x, cos, sin_signed, lane):
    ahead = pltpu.roll(x, LANES - DH_B // 4, axis=1)
    behind = pltpu.roll(x, DH_B // 4, axis=1)
    even_quarter = (lane // (DH_B // 4)) % 2 == 0
    return x * cos + jnp.where(even_quarter, ahead, behind) * sin_signed


def _ones_column(rows):
    lane = lax.broadcasted_iota(jnp.int32, (rows, LANES), 1)
    return jnp.where(lane == 0, 1.0, 0.0).astype(BF16)


def _lambda(lam_ref, lambda_init):
    lp = lam_ref[...]
    return (jnp.exp(jnp.sum(lp[0:1] * lp[1:2], axis=1, keepdims=True))
            - jnp.exp(jnp.sum(lp[2:3] * lp[3:4], axis=1, keepdims=True)) + lambda_init)


def _diff_attend(q, keys, v_aug, lam, dn, lambda_init):
    lane = lax.broadcasted_iota(jnp.int32, (1, LANES), 1)
    q = q * (DH_B ** -0.5 * LOG2_E)

    def softmax_pv(s):
        ex = jnp.exp2(s - jnp.max(s, axis=1, keepdims=True)).astype(BF16)
        r = _dot(ex, v_aug)
        return r[:, :DV_B] * (1.0 / r[:, DV_B:DV_B + 1])

    s1 = _dot_nt(jnp.where(lane < DH_B, q, 0.0).astype(BF16), keys)
    s2 = _dot_nt(jnp.where(lane >= DH_B, q, 0.0).astype(BF16), keys)
    o = softmax_pv(s1) - lam * softmax_pv(s2)
    ms = jnp.mean(o * o, axis=1, keepdims=True)
    return (o * lax.rsqrt(ms + EPS) * dn) * (1.0 - lambda_init)


def _attn_latent_kernel(q_ref, k_ref, v_ref, ck_ref, cv_ref, cos_ref, sin_ref, lam_ref, dn_ref,
                        o_ref, k_scr, v_scr, *, tq, sub, seq, lambda_init):
    qi = pl.program_id(2)
    lane = lax.broadcasted_iota(jnp.int32, (1, LANES), 1)

    @pl.when(qi == 0)
    def _():
        k_scr[0:seq, :] = _rope(k_ref[...], cos_ref[...], sin_ref[...], lane).astype(BF16)
        k_scr[seq:, :] = ck_ref[0, 0].astype(BF16)
        v_scr[0:seq, 0:DV_B] = v_ref[...].astype(BF16)
        v_scr[seq:, 0:DV_B] = cv_ref[0, 0].astype(BF16)
        v_scr[:, DV_B:] = _ones_column(v_scr.shape[0])

    lam = _lambda(lam_ref, lambda_init)
    for r0 in range(0, tq, sub):
        rows = pl.ds(pl.multiple_of(qi * tq + r0, sub), sub)
        q = _rope(q_ref[r0:r0 + sub, :], cos_ref[rows, :], sin_ref[rows, :], lane)
        o = _diff_attend(q, k_scr[...], v_scr[...], lam, dn_ref[...], lambda_init)
        o_ref[r0:r0 + sub, :] = o.astype(BF16)


def _attn_context_kernel(q_ref, k_ref, v_ref, lam_ref, dn_ref, o_ref, nk_ref, nv_ref, *,
                         lambda_init):
    lam = _lambda(lam_ref, lambda_init)
    ones = _ones_column(q_ref.shape[0])
    for h in range(H_B):
        cols = slice(h * LANES, (h + 1) * LANES)
        k = k_ref[:, cols]
        v = v_ref[:, cols]
        nk_ref[0, h] = k
        nv_ref[0, h] = v
        v_aug = jnp.concatenate([v.astype(BF16), ones], axis=1)
        o = _diff_attend(q_ref[:, cols], k.astype(BF16), v_aug, lam, dn_ref[...], lambda_init)
        o_ref[:, cols] = o.astype(BF16)


def _attention_context(qkv, lam_params, diff_norm, batch, seq, lambda_init):
    w = H_B * LANES
    return pl.pallas_call(
        functools.partial(_attn_context_kernel, lambda_init=lambda_init),
        grid=(batch,),
        in_specs=[pl.BlockSpec((seq, w), lambda b: (b, 0)),
                  pl.BlockSpec((seq, w), lambda b: (b, 1)),
                  pl.BlockSpec((seq, w), lambda b: (b, 2)),
                  pl.BlockSpec(lam_params.shape, lambda b: (0, 0)),
                  pl.BlockSpec((1, DV_B), lambda b: (0, 0))],
        out_specs=[pl.BlockSpec((seq, w), lambda b: (b, 0)),
                   pl.BlockSpec((1, H_B, seq, LANES), lambda b: (b, 0, 0, 0)),
                   pl.BlockSpec((1, H_B, seq, DV_B), lambda b: (b, 0, 0, 0))],
        out_shape=[jax.ShapeDtypeStruct((batch * seq, w), BF16),
                   jax.ShapeDtypeStruct((batch, H_B, seq, LANES), F32),
                   jax.ShapeDtypeStruct((batch, H_B, seq, DV_B), F32)],
        compiler_params=_params("arbitrary"),
        name="attention_context",
    )(qkv, qkv, qkv, lam_params, diff_norm)


def _attention_latent(qkv, lam_params, diff_norm, batch, seq, tq, lambda_init, cache, rope_tables):
    ck, cv = cache
    past = ck.shape[2]
    cos, sin_signed = rope_tables
    nq = seq // tq
    return pl.pallas_call(
        functools.partial(_attn_latent_kernel, tq=tq, sub=min(256, tq), seq=seq,
                          lambda_init=lambda_init),
        grid=(batch, H_B, nq),
        in_specs=[pl.BlockSpec((tq, LANES), lambda b, h, i: (b * nq + i, h)),
                  pl.BlockSpec((seq, LANES), lambda b, h, i: (b, H_B + h)),
                  pl.BlockSpec((seq, LANES), lambda b, h, i: (b, 2 * H_B + h)),
                  pl.BlockSpec((1, 1, past, LANES), lambda b, h, i: (b, h, 0, 0)),
                  pl.BlockSpec((1, 1, past, LANES), lambda b, h, i: (b, h, 0, 0)),
                  pl.BlockSpec((seq, LANES), lambda b, h, i: (0, 0)),
                  pl.BlockSpec((seq, LANES), lambda b, h, i: (0, 0)),
                  pl.BlockSpec(lam_params.shape, lambda b, h, i: (0, 0)),
                  pl.BlockSpec((1, DV_B), lambda b, h, i: (0, 0))],
        out_specs=pl.BlockSpec((tq, DV_B), lambda b, h, i: (b * nq + i, h)),
        out_shape=jax.ShapeDtypeStruct((batch * seq, H_B * DV_B), BF16),
        scratch_shapes=[pltpu.VMEM((seq + past, LANES), BF16),
                        pltpu.VMEM((seq + past, 2 * DV_B), BF16)],
        compiler_params=_params("arbitrary", "arbitrary", "arbitrary"),
        name="attention_latent",
    )(qkv, qkv, qkv, ck, cv, cos, sin_signed, lam_params, diff_norm)


def _merge_kernel(ya_ref, yb_ref, ga_ref, gb_ref, x_ref, mod_ref, wpa_ref, wpb_ref, wo_ref, o_ref):
    ya = _dot(ya_ref[...], wpa_ref[...])
    yb = _dot(yb_ref[...], wpb_ref[...])
    y = _sigmoid(ga_ref[...].astype(F32)) * ya + _sigmoid(gb_ref[...].astype(F32)) * yb
    z = _dot(y.astype(BF16), wo_ref[...])
    o_ref[...] = x_ref[...] + mod_ref[0, 2:3, :] * z


def _merge_project(y_a, y_b, proj, x2d, mod3, mod_row, w_pa, w_pb, w_out, tm=512):
    t, d = x2d.shape
    wa = y_a.shape[1]
    wb = y_b.shape[1]
    const = dict(pipeline_mode=pl.Buffered(1))
    return pl.pallas_call(
        _merge_kernel,
        grid=(t // tm,),
        in_specs=[pl.BlockSpec((tm, wa), lambda i: (i, 0)),
                  pl.BlockSpec((tm, wb), lambda i: (i, 0)),
                  pl.BlockSpec((tm, d), lambda i: (i, 0)),
                  pl.BlockSpec((tm, d), lambda i: (i, 1)),
                  pl.BlockSpec((tm, d), lambda i: (i, 0)),
                  pl.BlockSpec((1,) + mod3.shape[1:], lambda i: (mod_row(i * tm), 0, 0)),
                  pl.BlockSpec((wa, d), lambda i: (0, 0), **const),
                  pl.BlockSpec((wb, d), lambda i: (0, 0), **const),
                  pl.BlockSpec((d, d), lambda i: (0, 0), **const)],
        out_specs=pl.BlockSpec((tm, d), lambda i: (i, 0)),
        out_shape=jax.ShapeDtypeStruct((t, d), F32),
        compiler_params=_params("arbitrary"),
        name="merge_project",
    )(y_a, y_b, proj, proj, x2d, mod3, w_pa, w_pb, w_out)


def _ffn_kernel(x_ref, xp_ref, xn_ref, mod_ref, n2_ref, wa_ref, wb_ref, cwa_ref, cwb_ref,
                cba_ref, cbb_ref, wd_ref, fn_ref, o_ref, h_scr, acc_scr, ua_scr, ub_scr,
                *, tm, seq):
    i = pl.program_id(0)
    j = pl.program_id(1)
    halo = BF16_ROWS

    @pl.when(j == 0)
    def _():
        def nrm(x):
            return _modulated_norm(x, n2_ref[...], mod_ref[0, 4:5, :], mod_ref[0, 3:4, :])
        h_scr[0:halo, :] = nrm(xp_ref[...]).astype(BF16)
        h_scr[halo:halo + tm, :] = nrm(x_ref[...]).astype(BF16)
        h_scr[halo + tm:, :] = nrm(xn_ref[...]).astype(BF16)
        acc_scr[...] = jnp.zeros(acc_scr.shape, F32)

    h = h_scr[...]
    pos = (i * tm + lax.broadcasted_iota(jnp.int32, (tm, 1), 0)) % seq
    has_prev = jnp.where(pos != 0, 1.0, 0.0)
    has_next = jnp.where(pos != seq - 1, 1.0, 0.0)

    ua_scr[...] = _dot(h, wa_ref[...])
    ub_scr[...] = _dot(h, wb_ref[...])

    def conv(u_scr, cw_ref, cb_ref):
        prev = u_scr[halo - 1:halo - 1 + tm, :] * has_prev
        cur = u_scr[halo:halo + tm, :]
        nxt = u_scr[halo + 1:halo + 1 + tm, :] * has_next
        return prev * cw_ref[0:1, :] + cur * cw_ref[1:2, :] + nxt * cw_ref[2:3, :] + cb_ref[...]

    a = conv(ua_scr, cwa_ref, cba_ref)
    b = conv(ub_scr, cwb_ref, cbb_ref)
    g = (a * _sigmoid(a)) * b
    acc_scr[...] += _dot(g.astype(BF16), wd_ref[...])

    @pl.when(j == pl.num_programs(1) - 1)
    def _():
        x2 = x_ref[...] + mod_ref[0, 5:6, :] * acc_scr[...]
        ms = jnp.mean(x2 * x2, axis=-1, keepdims=True)
        o_ref[...] = x2 * lax.rsqrt(ms + EPS) * fn_ref[...]


def _conv_ffn(x2d, mod3, mod_row, norm2, w_up, conv_w, conv_b, w_down, final_norm, seq,
              tm=512, tf=512):
    t, d = x2d.shape
    dff = w_down.shape[0]
    nf = dff // tf
    halo = BF16_ROWS
    nhalo = t // halo
    per = tm // halo
    return pl.pallas_call(
        functools.partial(_ffn_kernel, tm=tm, seq=seq),
        grid=(t // tm, nf),
        in_specs=[pl.BlockSpec((tm, d), lambda i, j: (i, 0)),
                  pl.BlockSpec((halo, d), lambda i, j: (jnp.maximum(i * per - 1, 0), 0)),
                  pl.BlockSpec((halo, d), lambda i, j: (jnp.minimum((i + 1) * per, nhalo - 1), 0)),
                  pl.BlockSpec((1,) + mod3.shape[1:], lambda i, j: (mod_row(i * tm), 0, 0)),
                  pl.BlockSpec((1, d), lambda i, j: (0, 0)),
                  pl.BlockSpec((d, tf), lambda i, j: (0, j)),
                  pl.BlockSpec((d, tf), lambda i, j: (0, nf + j)),
                  pl.BlockSpec((3, tf), lambda i, j: (0, j)),
                  pl.BlockSpec((3, tf), lambda i, j: (0, nf + j)),
                  pl.BlockSpec((1, tf), lambda i, j: (0, j)),
                  pl.BlockSpec((1, tf), lambda i, j: (0, nf + j)),
                  pl.BlockSpec((tf, d), lambda i, j: (j, 0)),
                  pl.BlockSpec((1, d), lambda i, j: (0, 0))],
        out_specs=pl.BlockSpec((tm, d), lambda i, j: (i, 0)),
        out_shape=jax.ShapeDtypeStruct((t, d), F32),
        scratch_shapes=[pltpu.VMEM((tm + 2 * halo, d), BF16), pltpu.VMEM((tm, d), F32),
                        pltpu.VMEM((tm + 2 * halo, tf), F32), pltpu.VMEM((tm + 2 * halo, tf), F32)],
        compiler_params=_params("arbitrary", "arbitrary"),
        name="conv_ffn",
    )(x2d, x2d, x2d, mod3, norm2, w_up, w_up, conv_w, conv_w, conv_b, conv_b, w_down, final_norm)


def _rope_tables(n_tokens):
    rows = n_tokens // GRID_W
    row = jnp.repeat(jnp.arange(rows), GRID_W).astype(F32)
    col = jnp.tile(jnp.arange(GRID_W), rows).astype(F32)
    quarter = DH_B // 4
    inv_freq = jnp.power(ROPE_BASE, -jnp.arange(quarter, dtype=F32) / quarter)
    ang_r = row[:, None] * inv_freq
    ang_c = col[:, None] * inv_freq
    ang = jnp.concatenate([ang_r, ang_r, ang_c, ang_c] * 2, axis=-1)
    sign = jnp.where((jnp.arange(2 * DH_B) // quarter) % 2 == 0, -1.0, 1.0).astype(F32)
    return jnp.cos(ang), jnp.sin(ang) * sign


def _gate_columns(g):
    lead = g.shape[:-1]
    g4 = g.reshape(lead + (4, H_A))
    pad = jnp.zeros(lead + (LANES - 2 * H_A,), g.dtype)
    return jnp.concatenate([g4[..., 0, :], g4[..., 2, :], pad, g4[..., 1, :], g4[..., 3, :], pad],
                           axis=-1)


def _trunk(x2d, batch, seq, mod3, mod_row, wts, lambda_init, chunk, cache=None, state=None,
           rope_tables=None, emit_state=False):
    t, d = x2d.shape
    span = t if cache is None else seq
    proj, qkv, gates = _in_projection(x2d, mod3, mod_row, wts["norm1"], wts["w_main"],
                                      wts["w_gates"], cols16=wts["cols16"], tm=min(1024, span))
    a_out = _mlstm(proj, 2 * d, gates, wts["b_gates"], wts["mlstm_norm"], batch, seq, chunk,
                   state=state, emit_state=emit_state)
    if cache is None:
        b_out = _attention_context(qkv, wts["lam"], wts["diff_norm"], batch, seq, lambda_init)
    else:
        b_out = [_attention_latent(qkv, wts["lam"], wts["diff_norm"], batch, seq, min(1024, seq), lambda_init,
                                   cache, rope_tables)]
    x1 = _merge_project(a_out[0], b_out[0], proj, x2d, mod3, mod_row, wts["w_pa"], wts["w_pb"],
                        wts["w_out"], tm=min(512, span))
    y = _conv_ffn(x1, mod3, mod_row, wts["norm2"], wts["w_up"], wts["conv_w"], wts["conv_b"],
                  wts["w_down"], wts["final_norm"], seq, tm=min(512, span))
    return y, a_out[1:], b_out[1:]


def kernel(x_prompt, x_sample, cache_k, cache_v, state_C, state_n, state_m, c, c_ctx, w_mod, b_mod, norm1, w_in, b_gates, mlstm_norm, lam_q1, lam_k1, lam_q2, lam_k2, diff_norm, w_pa, w_pb, w_out, norm2, w_up, conv_w, conv_b, w_down, final_norm):
    assert w_in.shape[0] == 1, "single trunk layer"
    bp, sp, d = x_prompt.shape
    bs, ss, _ = x_sample.shape
    past = cache_k.shape[3]
    lambda_init = 0.8 - 0.6 * math.exp(-0.3 * 0)

    wa_cols = 2 * H_A * DQK_A + 2 * H_A * DV_A
    wb_cols = 3 * H_B * 2 * DH_B
    g0 = wa_cols
    b0 = g0 + 4 * H_A
    m0 = b0 + wb_cols
    w = w_in.reshape(w_in.shape[1:])
    wts = {
        "w_main": jnp.concatenate([w[:, m0:].astype(BF16), w[:, :g0].astype(BF16),
                                   w[:, b0:m0].astype(BF16)], axis=1),
        "cols16": 2 * d + wa_cols,
        "w_gates": _gate_columns(w[:, g0:b0]).astype(BF16),
        "b_gates": _gate_columns(b_gates.reshape(-1))[None, :],
        "norm1": norm1, "norm2": norm2, "mlstm_norm": mlstm_norm, "diff_norm": diff_norm,
        "lam": jnp.concatenate([lam_q1, lam_k1, lam_q2, lam_k2], axis=0),
        "w_pa": w_pa.reshape(w_pa.shape[1:]).astype(BF16),
        "w_pb": w_pb.reshape(w_pb.shape[1:]).astype(BF16),
        "w_out": w_out.reshape(w_out.shape[1:]).astype(BF16),
        "w_up": w_up.reshape(w_up.shape[1:]).astype(BF16),
        "conv_w": conv_w.reshape(conv_w.shape[1:]), "conv_b": conv_b,
        "w_down": w_down.reshape(w_down.shape[1:]).astype(BF16),
        "final_norm": final_norm[None, :],
    }

    c_all = jnp.concatenate([c_ctx[None, :], c, jnp.zeros((8 - 1 - bs, d), F32)], axis=0)
    mod3 = _modulation(c_all, w_mod.reshape(w_mod.shape[1:]), b_mod).reshape(8, 6, d)

    yp, (c_new, n_new, m_new), (k_new, v_new) = _trunk(
        x_prompt.reshape(bp * sp, d), bp, sp, mod3, lambda t: 0, wts, lambda_init,
        chunk=sp, emit_state=True)

    state = (state_C.reshape(bs, 2, H_A, DQK_A, DV_A), state_n.reshape(bs, 2, H_A // 2, LANES),
             state_m.reshape(bs, 2, H_A))
    cache = (cache_k.reshape(bs, H_B, past, 2 * DH_B), cache_v.reshape(bs, H_B, past, DV_B))
    ys, _, _ = _trunk(
        x_sample.reshape(bs * ss, d), bs, ss, mod3, lambda t: 1 + t // ss, wts, lambda_init,
        chunk=256, cache=cache, state=state, rope_tables=_rope_tables(ss))

    return (yp.reshape(bp, sp, d), ys.reshape(bs, ss, d),
            k_new.reshape(bp, 1, H_B, sp, 2 * DH_B), v_new.reshape(bp, 1, H_B, sp, DV_B),
            c_new.reshape(bp, 1, 2, H_A, DQK_A, DV_A), n_new.reshape(bp, 1, 2, H_A, DQK_A),
            m_new.reshape(bp, 1, 2, H_A))
```

```python
import functools
import math

import jax
import jax.numpy as jnp
from jax import lax
from jax.experimental import pallas as pl
from jax.experimental.pallas import tpu as pltpu

F32 = jnp.float32
BF16 = jnp.bfloat16

GRID_W = 64
H_A = 8
DQK_A = 64
DV_A = 128
H_B = 8
DH_B = 64
DV_B = 128
ROPE_BASE = 10000.0
GATE_CAP = 15.0
EPS = 1e-6
NEG_BIG = -1e30
LOG2_E = 1.4426950408889634

LANES = 128
BF16_ROWS = 16
MXU_COLS = 256
VMEM_LIMIT = 56 * 1024 * 1024


def _sigmoid(x):
    return 1.0 / (1.0 + jnp.exp(-x))


def _dot(a, b):
    return jnp.dot(a, b, preferred_element_type=F32)


def _dot_nt(a, b):
    return lax.dot_general(a, b, (((1,), (1,)), ((), ())), preferred_element_type=F32)


def _dot_tn(a, b):
    return lax.dot_general(a, b, (((0,), (0,)), ((), ())), preferred_element_type=F32)


def _tile(n, want):
    t = min(want, n)
    while n % t:
        t -= LANES
    return t


def _params(*sem):
    return pltpu.CompilerParams(dimension_semantics=sem, vmem_limit_bytes=VMEM_LIMIT)


def _mod_kernel(c_ref, w_ref, b_ref, o_ref):
    c = c_ref[...]
    s = (c * _sigmoid(c)).astype(BF16)
    o_ref[...] = _dot(s, w_ref[...].astype(BF16)) + b_ref[...]


def _modulation(c_all, w_mod, b_mod, tn=1024):
    rows, d = c_all.shape
    n = w_mod.shape[1]
    tn = _tile(n, tn)
    return pl.pallas_call(
        _mod_kernel,
        grid=(n // tn,),
        in_specs=[pl.BlockSpec((rows, d), lambda j: (0, 0)),
                  pl.BlockSpec((d, tn), lambda j: (0, j)),
                  pl.BlockSpec((1, tn), lambda j: (0, j))],
        out_specs=pl.BlockSpec((rows, tn), lambda j: (0, j)),
        out_shape=jax.ShapeDtypeStruct((rows, n), F32),
        compiler_params=_params("arbitrary"),
        name="modulation",
    )(c_all, w_mod, b_mod)


def _modulated_norm(x, w, scale, shift):
    ms = jnp.mean(x * x, axis=-1, keepdims=True)
    return (x * lax.rsqrt(ms + EPS) * w) * (1.0 + scale) + shift


def _inproj_kernel(x_ref, mod_ref, n1_ref, w_ref, wg_ref, o16_ref, o32_ref, g_ref, h_scr, *, n16):
    j = pl.program_id(1)

    @pl.when(j == 0)
    def _():
        h = _modulated_norm(x_ref[...], n1_ref[...], mod_ref[0, 1:2, :], mod_ref[0, 0:1, :])
        hb = h.astype(BF16)
        h_scr[...] = hb
        g_ref[...] = _dot(hb, wg_ref[...])

    @pl.when(j < n16)
    def _():
        o16_ref[...] = _dot(h_scr[...], w_ref[...]).astype(BF16)

    @pl.when(j >= n16)
    def _():
        o32_ref[...] = _dot(h_scr[...], w_ref[...])


def _in_projection(x2d, mod3, mod_row, norm1, w_main, w_gates, cols16, tm=1024, tn=1024):
    t, d = x2d.shape
    n = w_main.shape[1]
    ng = w_gates.shape[1]
    tn = _tile(math.gcd(cols16, n - cols16), tn)
    n16 = cols16 // tn
    n32 = (n - cols16) // tn
    return pl.pallas_call(
        functools.partial(_inproj_kernel, n16=n16),
        grid=(t // tm, n16 + n32),
        in_specs=[pl.BlockSpec((tm, d), lambda i, j: (i, 0)),
                  pl.BlockSpec((1,) + mod3.shape[1:], lambda i, j: (mod_row(i * tm), 0, 0)),
                  pl.BlockSpec((1, d), lambda i, j: (0, 0)),
                  pl.BlockSpec((d, tn), lambda i, j: (0, j)),
                  pl.BlockSpec((d, ng), lambda i, j: (0, 0))],
        out_specs=[pl.BlockSpec((tm, tn), lambda i, j: (i, jnp.minimum(j, n16 - 1))),
                   pl.BlockSpec((tm, tn), lambda i, j: (i, jnp.maximum(j - n16, 0))),
                   pl.BlockSpec((tm, ng), lambda i, j: (i, 0))],
        out_shape=[jax.ShapeDtypeStruct((t, cols16), BF16),
                   jax.ShapeDtypeStruct((t, n - cols16), F32),
                   jax.ShapeDtypeStruct((t, ng), F32)],
        scratch_shapes=[pltpu.VMEM((tm, d), BF16)],
        compiler_params=_params("arbitrary", "arbitrary"),
        name="in_projection",
    )(x2d, mod3, norm1, w_main, w_gates)


def _split3(x):
    hi = x.astype(BF16)
    r = x - hi.astype(F32)
    mid = r.astype(BF16)
    lo = (r - mid.astype(F32)).astype(BF16)
    return hi, mid, lo


def _mlstm_kernel(*refs, seq, chunk, has_state, emit_state):
    q_ref, k_ref, v_ref, o_ref, g_ref, bg_ref, nrm_ref = refs[:7]
    pos = 7
    if has_state:
        c0_ref, n0_ref, m0_ref = refs[pos:pos + 3]
        pos += 3
    y_ref = refs[pos]
    pos += 1
    if emit_state:
        co_ref, no_ref, mo_ref = refs[pos:pos + 3]
        pos += 3
    bc_scr, a_scr, at_scr, hacc_scr = refs[pos:pos + 4]

    L = chunk
    nc = seq // L
    p = pl.program_id(1)

    @pl.when(p == 0)
    def _():
        row = lax.broadcasted_iota(jnp.int32, (L, L), 0)
        col = lax.broadcasted_iota(jnp.int32, (L, L), 1)
        tril = jnp.where(col <= row, 1.0, 0.0).astype(BF16)
        triu = jnp.where(col >= row, 1.0, 0.0).astype(BF16)
        lane = lax.broadcasted_iota(jnp.int32, (L, LANES), 1)
        for c in range(nc):
            rows = pl.ds(c * L, L)
            g = g_ref[rows, :] + bg_ref[...]
            g = GATE_CAP * jnp.tanh(g / GATE_CAP)
            gi = g[:, :LANES]
            gf = g[:, LANES:]
            lf = jnp.minimum(gf, 0.0) - jnp.log(1.0 + jnp.exp(-jnp.abs(gf)))
            hi, mid, lo = _split3(lf)
            pre = _dot(tril, hi) + _dot(tril, mid) + _dot(tril, lo)
            suf = _dot(triu, hi) + _dot(triu, mid) + _dot(triu, lo)
            bc = jnp.where(lane < H_A, pre, suf)
            a = gi - bc
            bc_scr[rows, :] = bc
            a_scr[rows, :] = a
            at_scr[c] = a.T
        if emit_state:
            mo_ref[...] = jnp.zeros(mo_ref.shape, F32)

    lane_row = lax.broadcasted_iota(jnp.int32, (1, LANES), 1)
    lane_l = lax.broadcasted_iota(jnp.int32, (L, LANES), 1)
    trow = lax.broadcasted_iota(jnp.int32, (L, L), 0)
    tcol = lax.broadcasted_iota(jnp.int32, (L, L), 1)
    m_rows = lax.broadcasted_iota(jnp.int32, (2, H_A), 0)
    m_cols = lax.broadcasted_iota(jnp.int32, (2, H_A), 1)

    chains = [(e, d) for e in (0, 1) for d in (0, 1)]

    def head_mask(e):
        return (lane_row >= DQK_A) if e else (lane_row < DQK_A)

    def initial_state(e, d):
        if not has_state:
            return (jnp.zeros((2 * DQK_A, DV_A), F32), jnp.zeros((1, LANES), F32),
                    jnp.zeros((1, 1), F32))
        c0 = c0_ref[0, d, e]
        z = jnp.zeros_like(c0)
        sel_m = (m_rows == d) & (m_cols == 2 * p + e)
        m0 = jnp.sum(jnp.sum(jnp.where(sel_m, m0_ref[0], 0.0), axis=1, keepdims=True),
                     axis=0, keepdims=True)
        return (jnp.concatenate([z, c0] if e else [c0, z], axis=0),
                jnp.where(head_mask(e), n0_ref[0, d, pl.ds(p, 1), :], 0.0), m0)

    def chunk_rows(c):
        r0 = c * L
        return pl.ds(r0 if isinstance(r0, int) else pl.multiple_of(r0, L), L)

    def chunk_step(ci, carry, e, d):
        cst, nrow, m = carry
        c = ci if d == 0 else nc - 1 - ci
        rows = chunk_rows(c)
        hmask = head_mask(e)
        vcols = slice(e * DV_A, (e + 1) * DV_A)
        colidx = 2 * p + e + H_A * d
        causal = (tcol <= trow) if d == 0 else (tcol >= trow)
        qm = jnp.where(hmask, q_ref[rows, :].astype(F32) * (DQK_A ** -0.5), 0.0)
        qmb = qm.astype(BF16)
        kb = k_ref[rows, :]
        vh = v_ref[rows, vcols]
        s = _dot_nt(qmb, kb)
        pick = lane_l == colidx
        bcol = jnp.sum(jnp.where(pick, bc_scr[rows, :], 0.0), axis=1, keepdims=True)
        acol = jnp.sum(jnp.where(pick, a_scr[rows, :], 0.0), axis=1, keepdims=True)
        arow = at_scr[c, pl.ds(colidx, 1), :]
        logd = jnp.where(causal, bcol + arow, NEG_BIG)
        m_inter = bcol + m
        m_t = jnp.maximum(m_inter, jnp.max(logd, axis=1, keepdims=True))
        pmat = s * jnp.exp(logd - m_t)
        inter = jnp.exp(m_inter - m_t)
        num = _dot(pmat.astype(BF16), vh) + inter * _dot(qmb, cst.astype(BF16))
        den = (jnp.sum(pmat, axis=1, keepdims=True)
               + inter * jnp.sum(qm * nrow, axis=1, keepdims=True))
        hacc_scr[d, e, rows, :] = num / jnp.maximum(jnp.abs(den), jnp.exp(-m_t))
        b_last = bcol[L - 1:L, :] if d == 0 else bcol[0:1, :]
        logw = b_last + acol
        m_new = jnp.maximum(b_last + m, jnp.max(logw, axis=0, keepdims=True))
        wk = jnp.exp(logw - m_new) * jnp.where(hmask, kb.astype(F32), 0.0)
        decay = jnp.exp(b_last + m - m_new)
        cst = decay * cst + _dot_tn(wk.astype(BF16), vh)
        nrow = decay * nrow + jnp.sum(wk, axis=0, keepdims=True)
        return cst, nrow, m_new

    def all_chains(ci, carry):
        return tuple(chunk_step(ci, st, e, d) for (e, d), st in zip(chains, carry))

    def finish(c, _):
        rows = chunk_rows(c)
        for e in (0, 1):
            vcols = slice(e * DV_A, (e + 1) * DV_A)
            ht = hacc_scr[0, e, rows, :] + hacc_scr[1, e, rows, :]
            ms = jnp.mean(ht * ht, axis=1, keepdims=True)
            hn = ht * lax.rsqrt(ms + EPS) * nrm_ref[:, vcols]
            og = o_ref[rows, vcols].astype(F32)
            y_ref[rows, vcols] = (hn * _sigmoid(og)).astype(BF16)
        return 0

    init = tuple(initial_state(e, d) for e, d in chains)
    if nc == 1:
        final = all_chains(0, init)
        finish(0, 0)
    else:
        final = lax.fori_loop(0, nc, all_chains, init)
        lax.fori_loop(0, nc, finish, 0)

    if emit_state:
        state = dict(zip(chains, final))
        for (e, d), (cst, _, m) in state.items():
            co_ref[0, d, e] = cst[e * DQK_A:(e + 1) * DQK_A, :]
            sel_m = (m_rows == d) & (m_cols == 2 * p + e)
            mo_ref[0] = jnp.where(sel_m, m, mo_ref[0])
        for d in (0, 1):
            no_ref[0, d, pl.ds(p, 1), :] = state[(0, d)][1] + state[(1, d)][1]


def _mlstm(proj, col0, gates, b_gates2, mlstm_norm, batch, seq, chunk, state=None,
           emit_state=False):
    npairs = H_A // 2
    qblk = col0 // LANES
    kblk = qblk + (H_A * DQK_A) // LANES
    vblk = (col0 + 2 * H_A * DQK_A) // (2 * DV_A)
    oblk = vblk + (H_A * DV_A) // (2 * DV_A)
    ng = gates.shape[1]
    in_specs = [pl.BlockSpec((seq, LANES), lambda b, p: (b, qblk + p)),
                pl.BlockSpec((seq, LANES), lambda b, p: (b, kblk + p)),
                pl.BlockSpec((seq, 2 * DV_A), lambda b, p: (b, vblk + p)),
                pl.BlockSpec((seq, 2 * DV_A), lambda b, p: (b, oblk + p)),
                pl.BlockSpec((seq, ng), lambda b, p: (b, 0)),
                pl.BlockSpec((1, ng), lambda b, p: (0, 0)),
                pl.BlockSpec((1, 2 * DV_A), lambda b, p: (0, p))]
    args = [proj, proj, proj, proj, gates, b_gates2, mlstm_norm]
    if state is not None:
        c0, n0, m0 = state
        in_specs += [pl.BlockSpec((1, 2, 2, DQK_A, DV_A), lambda b, p: (b, 0, p, 0, 0)),
                     pl.BlockSpec((1, 2, npairs, LANES), lambda b, p: (b, 0, 0, 0)),
                     pl.BlockSpec((1, 2, H_A), lambda b, p: (b, 0, 0))]
        args += [c0, n0, m0]
    out_specs = [pl.BlockSpec((seq, 2 * DV_A), lambda b, p: (b, p))]
    out_shape = [jax.ShapeDtypeStruct((batch * seq, H_A * DV_A), BF16)]
    if emit_state:
        out_specs += [pl.BlockSpec((1, 2, 2, DQK_A, DV_A), lambda b, p: (b, 0, p, 0, 0)),
                      pl.BlockSpec((1, 2, npairs, LANES), lambda b, p: (b, 0, 0, 0)),
                      pl.BlockSpec((1, 2, H_A), lambda b, p: (b, 0, 0))]
        out_shape += [jax.ShapeDtypeStruct((batch, 2, H_A, DQK_A, DV_A), F32),
                      jax.ShapeDtypeStruct((batch, 2, npairs, LANES), F32),
                      jax.ShapeDtypeStruct((batch, 2, H_A), F32)]
    return pl.pallas_call(
        functools.partial(_mlstm_kernel, seq=seq, chunk=chunk, has_state=state is not None,
                          emit_state=emit_state),
        grid=(batch, npairs),
        in_specs=in_specs,
        out_specs=out_specs,
        out_shape=out_shape,
        scratch_shapes=[pltpu.VMEM((seq, LANES), F32), pltpu.VMEM((seq, LANES), F32),
                        pltpu.VMEM((seq // chunk, LANES, chunk), F32),
                        pltpu.VMEM((2, 2, seq, DV_A), F32)],
        compiler_params=_params("arbitrary", "arbitrary"),
        name="mlstm",
    )(*args)


def _rope(x, cos, sin_signed, lane):
    ahead = pltpu.roll(x, LANES - DH_B // 4, axis=1)
    behind = pltpu.roll(x, DH_B // 4, axis=1)
    even_quarter = (lane // (DH_B // 4)) % 2 == 0
    return x * cos + jnp.where(even_quarter, ahead, behind) * sin_signed


def _ones_column(rows):
    lane = lax.broadcasted_iota(jnp.int32, (rows, LANES), 1)
    return jnp.where(lane == 0, 1.0, 0.0).astype(BF16)


def _lambda(lam_ref, lambda_init):
    lp = lam_ref[...]
    return (jnp.exp(jnp.sum(lp[0:1] * lp[1:2], axis=1, keepdims=True))
            - jnp.exp(jnp.sum(lp[2:3] * lp[3:4], axis=1, keepdims=True)) + lambda_init)


def _diff_attend(q, keys, v_aug, lam, dn, lambda_init):
    lane = lax.broadcasted_iota(jnp.int32, (1, LANES), 1)
    q = q * (DH_B ** -0.5 * LOG2_E)

    def softmax_pv(s):
        ex = jnp.exp2(s - jnp.max(s, axis=1, keepdims=True)).astype(BF16)
        r = _dot(ex, v_aug)
        return r[:, :DV_B] * (1.0 / r[:, DV_B:DV_B + 1])

    s1 = _dot_nt(jnp.where(lane < DH_B, q, 0.0).astype(BF16), keys)
    s2 = _dot_nt(jnp.where(lane >= DH_B, q, 0.0).astype(BF16), keys)
    o = softmax_pv(s1) - lam * softmax_pv(s2)
    ms = jnp.mean(o * o, axis=1, keepdims=True)
    return (o * lax.rsqrt(ms + EPS) * dn) * (1.0 - lambda_init)


def _attn_latent_kernel(q_ref, k_ref, v_ref, ck_ref, cv_ref, cos_ref, sin_ref, lam_ref, dn_ref,
                        o_ref, k_scr, v_scr, *, tq, sub, seq, lambda_init):
    qi = pl.program_id(2)
    lane = lax.broadcasted_iota(jnp.int32, (1, LANES), 1)

    @pl.when(qi == 0)
    def _():
        k_scr[0:seq, :] = _rope(k_ref[...], cos_ref[...], sin_ref[...], lane).astype(BF16)
        k_scr[seq:, :] = ck_ref[0, 0].astype(BF16)
        v_scr[0:seq, 0:DV_B] = v_ref[...].astype(BF16)
        v_scr[seq:, 0:DV_B] = cv_ref[0, 0].astype(BF16)
        v_scr[:, DV_B:] = _ones_column(v_scr.shape[0])

    lam = _lambda(lam_ref, lambda_init)
    for r0 in range(0, tq, sub):
        rows = pl.ds(pl.multiple_of(qi * tq + r0, sub), sub)
        q = _rope(q_ref[r0:r0 + sub, :], cos_ref[rows, :], sin_ref[rows, :], lane)
        o = _diff_attend(q, k_scr[...], v_scr[...], lam, dn_ref[...], lambda_init)
        o_ref[r0:r0 + sub, :] = o.astype(BF16)


def _attn_context_kernel(q_ref, k_ref, v_ref, lam_ref, dn_ref, o_ref, nk_ref, nv_ref, *,
                         lambda_init):
    lam = _lambda(lam_ref, lambda_init)
    ones = _ones_column(q_ref.shape[0])
    for h in range(H_B):
        cols = slice(h * LANES, (h + 1) * LANES)
        k = k_ref[:, cols]
        v = v_ref[:, cols]
        nk_ref[0, h] = k
        nv_ref[0, h] = v
        v_aug = jnp.concatenate([v.astype(BF16), ones], axis=1)
        o = _diff_attend(q_ref[:, cols], k.astype(BF16), v_aug, lam, dn_ref[...], lambda_init)
        o_ref[:, cols] = o.astype(BF16)


def _attention_context(qkv, lam_params, diff_norm, batch, seq, lambda_init):
    w = H_B * LANES
    return pl.pallas_call(
        functools.partial(_attn_context_kernel, lambda_init=lambda_init),
        grid=(batch,),
        in_specs=[pl.BlockSpec((seq, w), lambda b: (b, 0)),
                  pl.BlockSpec((seq, w), lambda b: (b, 1)),
                  pl.BlockSpec((seq, w), lambda b: (b, 2)),
                  pl.BlockSpec(lam_params.shape, lambda b: (0, 0)),
                  pl.BlockSpec((1, DV_B), lambda b: (0, 0))],
        out_specs=[pl.BlockSpec((seq, w), lambda b: (b, 0)),
                   pl.BlockSpec((1, H_B, seq, LANES), lambda b: (b, 0, 0, 0)),
                   pl.BlockSpec((1, H_B, seq, DV_B), lambda b: (b, 0, 0, 0))],
        out_shape=[jax.ShapeDtypeStruct((batch * seq, w), BF16),
                   jax.ShapeDtypeStruct((batch, H_B, seq, LANES), F32),
                   jax.ShapeDtypeStruct((batch, H_B, seq, DV_B), F32)],
        compiler_params=_params("arbitrary"),
        name="attention_context",
    )(qkv, qkv, qkv, lam_params, diff_norm)


def _attention_latent(qkv, lam_params, diff_norm, batch, seq, tq, lambda_init, cache, rope_tables):
    ck, cv = cache
    past = ck.shape[2]
    cos, sin_signed = rope_tables
    nq = seq // tq
    return pl.pallas_call(
        functools.partial(_attn_latent_kernel, tq=tq, sub=min(256, tq), seq=seq,
                          lambda_init=lambda_init),
        grid=(batch, H_B, nq),
        in_specs=[pl.BlockSpec((tq, LANES), lambda b, h, i: (b * nq + i, h)),
                  pl.BlockSpec((seq, LANES), lambda b, h, i: (b, H_B + h)),
                  pl.BlockSpec((seq, LANES), lambda b, h, i: (b, 2 * H_B + h)),
                  pl.BlockSpec((1, 1, past, LANES), lambda b, h, i: (b, h, 0, 0)),
                  pl.BlockSpec((1, 1, past, LANES), lambda b, h, i: (b, h, 0, 0)),
                  pl.BlockSpec((seq, LANES), lambda b, h, i: (0, 0)),
                  pl.BlockSpec((seq, LANES), lambda b, h, i: (0, 0)),
                  pl.BlockSpec(lam_params.shape, lambda b, h, i: (0, 0)),
                  pl.BlockSpec((1, DV_B), lambda b, h, i: (0, 0))],
        out_specs=pl.BlockSpec((tq, DV_B), lambda b, h, i: (b * nq + i, h)),
        out_shape=jax.ShapeDtypeStruct((batch * seq, H_B * DV_B), BF16),
        scratch_shapes=[pltpu.VMEM((seq + past, LANES), BF16),
                        pltpu.VMEM((seq + past, 2 * DV_B), BF16)],
        compiler_params=_params("arbitrary", "arbitrary", "arbitrary"),
        name="attention_latent",
    )(qkv, qkv, qkv, ck, cv, cos, sin_signed, lam_params, diff_norm)


def _merge_kernel(ya_ref, yb_ref, ga_ref, gb_ref, x_ref, mod_ref, wpa_ref, wpb_ref, wo_ref, o_ref):
    ya = _dot(ya_ref[...], wpa_ref[...])
    yb = _dot(yb_ref[...], wpb_ref[...])
    y = _sigmoid(ga_ref[...].astype(F32)) * ya + _sigmoid(gb_ref[...].astype(F32)) * yb
    z = _dot(y.astype(BF16), wo_ref[...])
    o_ref[...] = x_ref[...] + mod_ref[0, 2:3, :] * z


def _merge_project(y_a, y_b, proj, x2d, mod3, mod_row, w_pa, w_pb, w_out, tm=512):
    t, d = x2d.shape
    wa = y_a.shape[1]
    wb = y_b.shape[1]
    const = dict(pipeline_mode=pl.Buffered(1))
    return pl.pallas_call(
        _merge_kernel,
        grid=(t // tm,),
        in_specs=[pl.BlockSpec((tm, wa), lambda i: (i, 0)),
                  pl.BlockSpec((tm, wb), lambda i: (i, 0)),
                  pl.BlockSpec((tm, d), lambda i: (i, 0)),
                  pl.BlockSpec((tm, d), lambda i: (i, 1)),
                  pl.BlockSpec((tm, d), lambda i: (i, 0)),
                  pl.BlockSpec((1,) + mod3.shape[1:], lambda i: (mod_row(i * tm), 0, 0)),
                  pl.BlockSpec((wa, d), lambda i: (0, 0), **const),
                  pl.BlockSpec((wb, d), lambda i: (0, 0), **const),
                  pl.BlockSpec((d, d), lambda i: (0, 0), **const)],
        out_specs=pl.BlockSpec((tm, d), lambda i: (i, 0)),
        out_shape=jax.ShapeDtypeStruct((t, d), F32),
        compiler_params=_params("arbitrary"),
        name="merge_project",
    )(y_a, y_b, proj, proj, x2d, mod3, w_pa, w_pb, w_out)


def _ffn_kernel(x_ref, xp_ref, xn_ref, mod_ref, n2_ref, wa_ref, wb_ref, cwa_ref, cwb_ref,
                cba_ref, cbb_ref, wd_ref, fn_ref, o_ref, h_scr, acc_scr, ua_scr, ub_scr,
                *, tm, seq):
    i = pl.program_id(0)
    j = pl.program_id(1)
    halo = BF16_ROWS

    @pl.when(j == 0)
    def _():
        def nrm(x):
            return _modulated_norm(x, n2_ref[...], mod_ref[0, 4:5, :], mod_ref[0, 3:4, :])
        starts_seq = (i * tm) % seq == 0
        ends_seq = ((i + 1) * tm) % seq == 0
        h_scr[0:halo, :] = jnp.where(starts_seq, 0.0, nrm(xp_ref[...])).astype(BF16)
        h_scr[halo:halo + tm, :] = nrm(x_ref[...]).astype(BF16)
        h_scr[halo + tm:, :] = jnp.where(ends_seq, 0.0, nrm(xn_ref[...])).astype(BF16)
        acc_scr[...] = jnp.zeros(acc_scr.shape, F32)

    h = h_scr[...]
    ua_scr[...] = _dot(h, wa_ref[...])
    ub_scr[...] = _dot(h, wb_ref[...])

    inner = range(seq, tm, seq)
    sub = lax.broadcasted_iota(jnp.int32, (8, 1), 0)

    def window(u_scr, shift, drop_rows, drop_sub):
        parts = []
        r0 = 0
        for r in drop_rows:
            slab = r // 8 * 8
            if slab > r0:
                parts.append(u_scr[halo + shift + r0:halo + shift + slab, :])
            parts.append(jnp.where(sub == drop_sub, 0.0,
                                   u_scr[halo + shift + slab:halo + shift + slab + 8, :]))
            r0 = slab + 8
        parts.append(u_scr[halo + shift + r0:halo + shift + tm, :])
        return parts[0] if len(parts) == 1 else jnp.concatenate(parts, axis=0)

    def conv(u_scr, cw_ref, cb_ref):
        prev = window(u_scr, -1, list(inner), 0)
        cur = u_scr[halo:halo + tm, :]
        nxt = window(u_scr, 1, [r - 1 for r in inner], 7)
        return prev * cw_ref[0:1, :] + cur * cw_ref[1:2, :] + nxt * cw_ref[2:3, :] + cb_ref[...]

    a = conv(ua_scr, cwa_ref, cba_ref)
    b = conv(ub_scr, cwb_ref, cbb_ref)
    g = (a * _sigmoid(a)) * b
    acc_scr[...] += _dot(g.astype(BF16), wd_ref[...])

    @pl.when(j == pl.num_programs(1) - 1)
    def _():
        x2 = x_ref[...] + mod_ref[0, 5:6, :] * acc_scr[...]
        ms = jnp.mean(x2 * x2, axis=-1, keepdims=True)
        o_ref[...] = x2 * lax.rsqrt(ms + EPS) * fn_ref[...]


def _conv_ffn(x2d, mod3, mod_row, norm2, w_up, conv_w, conv_b, w_down, final_norm, seq,
              tm=512, tf=512):
    t, d = x2d.shape
    dff = w_down.shape[0]
    nf = dff // tf
    halo = BF16_ROWS
    nhalo = t // halo
    per = tm // halo
    assert tm % seq == 0 or seq % tm == 0, "sequence boundaries must sit at static rows of a tile"
    return pl.pallas_call(
        functools.partial(_ffn_kernel, tm=tm, seq=seq),
        grid=(t // tm, nf),
        in_specs=[pl.BlockSpec((tm, d), lambda i, j: (i, 0)),
                  pl.BlockSpec((halo, d), lambda i, j: (jnp.maximum(i * per - 1, 0), 0)),
                  pl.BlockSpec((halo, d), lambda i, j: (jnp.minimum((i + 1) * per, nhalo - 1), 0)),
                  pl.BlockSpec((1,) + mod3.shape[1:], lambda i, j: (mod_row(i * tm), 0, 0)),
                  pl.BlockSpec((1, d), lambda i, j: (0, 0)),
                  pl.BlockSpec((d, tf), lambda i, j: (0, j)),
                  pl.BlockSpec((d, tf), lambda i, j: (0, nf + j)),
                  pl.BlockSpec((3, tf), lambda i, j: (0, j)),
                  pl.BlockSpec((3, tf), lambda i, j: (0, nf + j)),
                  pl.BlockSpec((1, tf), lambda i, j: (0, j)),
                  pl.BlockSpec((1, tf), lambda i, j: (0, nf + j)),
                  pl.BlockSpec((tf, d), lambda i, j: (j, 0)),
                  pl.BlockSpec((1, d), lambda i, j: (0, 0))],
        out_specs=pl.BlockSpec((tm, d), lambda i, j: (i, 0)),
        out_shape=jax.ShapeDtypeStruct((t, d), F32),
        scratch_shapes=[pltpu.VMEM((tm + 2 * halo, d), BF16), pltpu.VMEM((tm, d), F32),
                        pltpu.VMEM((tm + 2 * halo, tf), F32), pltpu.VMEM((tm + 2 * halo, tf), F32)],
        compiler_params=_params("arbitrary", "arbitrary"),
        name="conv_ffn",
    )(x2d, x2d, x2d, mod3, norm2, w_up, w_up, conv_w, conv_w, conv_b, conv_b, w_down, final_norm)


def _rope_tables(n_tokens):
    rows = n_tokens // GRID_W
    row = jnp.repeat(jnp.arange(rows), GRID_W).astype(F32)
    col = jnp.tile(jnp.arange(GRID_W), rows).astype(F32)
    quarter = DH_B // 4
    inv_freq = jnp.power(ROPE_BASE, -jnp.arange(quarter, dtype=F32) / quarter)
    ang_r = row[:, None] * inv_freq
    ang_c = col[:, None] * inv_freq
    ang = jnp.concatenate([ang_r, ang_r, ang_c, ang_c] * 2, axis=-1)
    sign = jnp.where((jnp.arange(2 * DH_B) // quarter) % 2 == 0, -1.0, 1.0).astype(F32)
    return jnp.cos(ang), jnp.sin(ang) * sign


def _gate_columns(g):
    lead = g.shape[:-1]
    g4 = g.reshape(lead + (4, H_A))
    pad = jnp.zeros(lead + (LANES - 2 * H_A,), g.dtype)
    return jnp.concatenate([g4[..., 0, :], g4[..., 2, :], pad, g4[..., 1, :], g4[..., 3, :], pad],
                           axis=-1)


def _trunk(x2d, batch, seq, mod3, mod_row, wts, lambda_init, chunk, cache=None, state=None,
           rope_tables=None, emit_state=False):
    t, d = x2d.shape
    span = t if cache is None else seq
    proj, qkv, gates = _in_projection(x2d, mod3, mod_row, wts["norm1"], wts["w_main"],
                                      wts["w_gates"], cols16=wts["cols16"], tm=min(1024, span))
    a_out = _mlstm(proj, 2 * d, gates, wts["b_gates"], wts["mlstm_norm"], batch, seq, chunk,
                   state=state, emit_state=emit_state)
    if cache is None:
        b_out = _attention_context(qkv, wts["lam"], wts["diff_norm"], batch, seq, lambda_init)
    else:
        b_out = [_attention_latent(qkv, wts["lam"], wts["diff_norm"], batch, seq, min(1024, seq), lambda_init,
                                   cache, rope_tables)]
    x1 = _merge_project(a_out[0], b_out[0], proj, x2d, mod3, mod_row, wts["w_pa"], wts["w_pb"],
                        wts["w_out"], tm=min(512, span))
    y = _conv_ffn(x1, mod3, mod_row, wts["norm2"], wts["w_up"], wts["conv_w"], wts["conv_b"],
                  wts["w_down"], wts["final_norm"], seq, tm=min(512, span))
    return y, a_out[1:], b_out[1:]


def kernel(x_prompt, x_sample, cache_k, cache_v, state_C, state_n, state_m, c, c_ctx, w_mod, b_mod, norm1, w_in, b_gates, mlstm_norm, lam_q1, lam_k1, lam_q2, lam_k2, diff_norm, w_pa, w_pb, w_out, norm2, w_up, conv_w, conv_b, w_down, final_norm):
    assert w_in.shape[0] == 1, "single trunk layer"
    bp, sp, d = x_prompt.shape
    bs, ss, _ = x_sample.shape
    past = cache_k.shape[3]
    lambda_init = 0.8 - 0.6 * math.exp(-0.3 * 0)

    wa_cols = 2 * H_A * DQK_A + 2 * H_A * DV_A
    wb_cols = 3 * H_B * 2 * DH_B
    g0 = wa_cols
    b0 = g0 + 4 * H_A
    m0 = b0 + wb_cols
    w = w_in.reshape(w_in.shape[1:])
    wts = {
        "w_main": jnp.concatenate([w[:, m0:].astype(BF16), w[:, :g0].astype(BF16),
                                   w[:, b0:m0].astype(BF16)], axis=1),
        "cols16": 2 * d + wa_cols,
        "w_gates": _gate_columns(w[:, g0:b0]).astype(BF16),
        "b_gates": _gate_columns(b_gates.reshape(-1))[None, :],
        "norm1": norm1, "norm2": norm2, "mlstm_norm": mlstm_norm, "diff_norm": diff_norm,
        "lam": jnp.concatenate([lam_q1, lam_k1, lam_q2, lam_k2], axis=0),
        "w_pa": w_pa.reshape(w_pa.shape[1:]).astype(BF16),
        "w_pb": w_pb.reshape(w_pb.shape[1:]).astype(BF16),
        "w_out": w_out.reshape(w_out.shape[1:]).astype(BF16),
        "w_up": w_up.reshape(w_up.shape[1:]).astype(BF16),
        "conv_w": conv_w.reshape(conv_w.shape[1:]), "conv_b": conv_b,
        "w_down": w_down.reshape(w_down.shape[1:]).astype(BF16),
        "final_norm": final_norm[None, :],
    }

    c_all = jnp.concatenate([c_ctx[None, :], c, jnp.zeros((8 - 1 - bs, d), F32)], axis=0)
    mod3 = _modulation(c_all, w_mod.reshape(w_mod.shape[1:]), b_mod).reshape(8, 6, d)

    yp, (c_new, n_new, m_new), (k_new, v_new) = _trunk(
        x_prompt.reshape(bp * sp, d), bp, sp, mod3, lambda t: 0, wts, lambda_init,
        chunk=sp, emit_state=True)

    state = (state_C.reshape(bs, 2, H_A, DQK_A, DV_A), state_n.reshape(bs, 2, H_A // 2, LANES),
             state_m.reshape(bs, 2, H_A))
    cache = (cache_k.reshape(bs, H_B, past, 2 * DH_B), cache_v.reshape(bs, H_B, past, DV_B))
    ys, _, _ = _trunk(
        x_sample.reshape(bs * ss, d), bs, ss, mod3, lambda t: 1 + t // ss, wts, lambda_init,
        chunk=256, cache=cache, state=state, rope_tables=_rope_tables(ss))

    return (yp.reshape(bp, sp, d), ys.reshape(bs, ss, d),
            k_new.reshape(bp, 1, H_B, sp, 2 * DH_B), v_new.reshape(bp, 1, H_B, sp, DV_B),
            c_new.reshape(bp, 1, 2, H_A, DQK_A, DV_A), n_new.reshape(bp, 1, 2, H_A, DQK_A),
            m_new.reshape(bp, 1, 2, H_A))
```

```python
import functools
import math

import jax
import jax.numpy as jnp
from jax import lax
from jax.experimental import pallas as pl
from jax.experimental.pallas import tpu as pltpu

F32 = jnp.float32
BF16 = jnp.bfloat16

GRID_W = 64
H_A = 8
DQK_A = 64
DV_A = 128
H_B = 8
DH_B = 64
DV_B = 128
ROPE_BASE = 10000.0
GATE_CAP = 15.0
EPS = 1e-6
NEG_BIG = -1e30
LOG2_E = 1.4426950408889634

LANES = 128
BF16_ROWS = 16
MXU_COLS = 256
VMEM_LIMIT = 56 * 1024 * 1024


def _sigmoid(x):
    return 1.0 / (1.0 + jnp.exp(-x))


def _dot(a, b):
    return jnp.dot(a, b, preferred_element_type=F32)


def _dot_nt(a, b):
    return lax.dot_general(a, b, (((1,), (1,)), ((), ())), preferred_element_type=F32)


def _dot_tn(a, b):
    return lax.dot_general(a, b, (((0,), (0,)), ((), ())), preferred_element_type=F32)


def _tile(n, want):
    t = min(want, n)
    while n % t:
        t -= LANES
    return t


def _params(*sem):
    return pltpu.CompilerParams(dimension_semantics=sem, vmem_limit_bytes=VMEM_LIMIT)


def _mod_kernel(c_ref, w_ref, b_ref, o_ref):
    c = c_ref[...]
    s = (c * _sigmoid(c)).astype(BF16)
    o_ref[...] = _dot(s, w_ref[...].astype(BF16)) + b_ref[...]


def _modulation(c_all, w_mod, b_mod, tn=1024):
    rows, d = c_all.shape
    n = w_mod.shape[1]
    tn = _tile(n, tn)
    return pl.pallas_call(
        _mod_kernel,
        grid=(n // tn,),
        in_specs=[pl.BlockSpec((rows, d), lambda j: (0, 0)),
                  pl.BlockSpec((d, tn), lambda j: (0, j)),
                  pl.BlockSpec((1, tn), lambda j: (0, j))],
        out_specs=pl.BlockSpec((rows, tn), lambda j: (0, j)),
        out_shape=jax.ShapeDtypeStruct((rows, n), F32),
        compiler_params=_params("arbitrary"),
        name="modulation",
    )(c_all, w_mod, b_mod)


def _modulated_norm(x, w, scale, shift):
    ms = jnp.mean(x * x, axis=-1, keepdims=True)
    return (x * lax.rsqrt(ms + EPS) * w) * (1.0 + scale) + shift


def _inproj_kernel(x_ref, mod_ref, n1_ref, wa_ref, wt_ref, wg_ref, o16_ref, o32_ref, g_ref, h_scr,
                   *, n_a, n_b):
    j = pl.program_id(1)

    @pl.when(j == 0)
    def _():
        h = _modulated_norm(x_ref[...], n1_ref[...], mod_ref[0, 1:2, :], mod_ref[0, 0:1, :])
        hb = h.astype(BF16)
        h_scr[...] = hb
        g_ref[...] = _dot(hb, wg_ref[...])

    @pl.when(j < n_a)
    def _():
        o16_ref[...] = _dot(h_scr[...], wa_ref[...]).astype(BF16)

    @pl.when((j >= n_a) & (j < n_a + n_b))
    def _():
        o32_ref[...] = _dot(h_scr[...], wt_ref[...])

    @pl.when(j >= n_a + n_b)
    def _():
        o16_ref[...] = _dot(h_scr[...], wt_ref[...]).astype(BF16)


def _in_projection(x2d, mod3, mod_row, norm1, w_full, w_tail, w_gates, cols_a, cols_b, tm=1024,
                   tn=1024):
    t, d = x2d.shape
    cols_g = w_tail.shape[1] - cols_b
    ng = w_gates.shape[1]
    tn = _tile(math.gcd(math.gcd(cols_a, cols_b), cols_g), tn)
    n_a, n_b, n_g = cols_a // tn, cols_b // tn, cols_g // tn

    def o16_block(j):
        return jnp.where(j < n_a + n_b, n_g + jnp.minimum(j, n_a - 1), j - (n_a + n_b))

    return pl.pallas_call(
        functools.partial(_inproj_kernel, n_a=n_a, n_b=n_b),
        grid=(t // tm, n_a + n_b + n_g),
        in_specs=[pl.BlockSpec((tm, d), lambda i, j: (i, 0)),
                  pl.BlockSpec((1,) + mod3.shape[1:], lambda i, j: (mod_row(i * tm), 0, 0)),
                  pl.BlockSpec((1, d), lambda i, j: (0, 0)),
                  pl.BlockSpec((d, tn), lambda i, j: (0, jnp.minimum(j, n_a - 1))),
                  pl.BlockSpec((d, tn), lambda i, j: (0, jnp.maximum(j - n_a, 0))),
                  pl.BlockSpec((d, ng), lambda i, j: (0, 0))],
        out_specs=[pl.BlockSpec((tm, tn), lambda i, j: (i, o16_block(j))),
                   pl.BlockSpec((tm, tn), lambda i, j: (i, jnp.clip(j - n_a, 0, n_b - 1))),
                   pl.BlockSpec((tm, ng), lambda i, j: (i, 0))],
        out_shape=[jax.ShapeDtypeStruct((t, cols_g + cols_a), BF16),
                   jax.ShapeDtypeStruct((t, cols_b), F32),
                   jax.ShapeDtypeStruct((t, ng), F32)],
        scratch_shapes=[pltpu.VMEM((tm, d), BF16)],
        compiler_params=_params("arbitrary", "arbitrary"),
        name="in_projection",
    )(x2d, mod3, norm1, w_full, w_tail, w_gates)


def _split3(x):
    hi = x.astype(BF16)
    r = x - hi.astype(F32)
    mid = r.astype(BF16)
    lo = (r - mid.astype(F32)).astype(BF16)
    return hi, mid, lo


def _mlstm_kernel(*refs, seq, chunk, has_state, emit_state):
    q_ref, k_ref, v_ref, o_ref, g_ref, bg_ref, nrm_ref = refs[:7]
    pos = 7
    if has_state:
        c0_ref, n0_ref, m0_ref = refs[pos:pos + 3]
        pos += 3
    y_ref = refs[pos]
    pos += 1
    if emit_state:
        co_ref, no_ref, mo_ref = refs[pos:pos + 3]
        pos += 3
    bc_scr, a_scr, at_scr, hacc_scr = refs[pos:pos + 4]

    L = chunk
    nc = seq // L
    p = pl.program_id(1)

    @pl.when(p == 0)
    def _():
        row = lax.broadcasted_iota(jnp.int32, (L, L), 0)
        col = lax.broadcasted_iota(jnp.int32, (L, L), 1)
        tril = jnp.where(col <= row, 1.0, 0.0).astype(BF16)
        triu = jnp.where(col >= row, 1.0, 0.0).astype(BF16)
        lane = lax.broadcasted_iota(jnp.int32, (L, LANES), 1)
        for c in range(nc):
            rows = pl.ds(c * L, L)
            g = g_ref[rows, :] + bg_ref[...]
            g = GATE_CAP * jnp.tanh(g / GATE_CAP)
            gi = g[:, :LANES]
            gf = g[:, LANES:]
            lf = jnp.minimum(gf, 0.0) - jnp.log(1.0 + jnp.exp(-jnp.abs(gf)))
            hi, mid, lo = _split3(lf)
            pre = _dot(tril, hi) + _dot(tril, mid) + _dot(tril, lo)
            suf = _dot(triu, hi) + _dot(triu, mid) + _dot(triu, lo)
            bc = jnp.where(lane < H_A, pre, suf)
            a = gi - bc
            bc_scr[rows, :] = bc
            a_scr[rows, :] = a
            at_scr[c] = a.T
        if emit_state:
            mo_ref[...] = jnp.zeros(mo_ref.shape, F32)

    lane_row = lax.broadcasted_iota(jnp.int32, (1, LANES), 1)
    lane_l = lax.broadcasted_iota(jnp.int32, (L, LANES), 1)
    trow = lax.broadcasted_iota(jnp.int32, (L, L), 0)
    tcol = lax.broadcasted_iota(jnp.int32, (L, L), 1)
    m_rows = lax.broadcasted_iota(jnp.int32, (2, H_A), 0)
    m_cols = lax.broadcasted_iota(jnp.int32, (2, H_A), 1)

    chains = [(e, d) for e in (0, 1) for d in (0, 1)]

    def head_mask(e):
        return (lane_row >= DQK_A) if e else (lane_row < DQK_A)

    def initial_state(e, d):
        if not has_state:
            return (jnp.zeros((2 * DQK_A, DV_A), F32), jnp.zeros((1, LANES), F32),
                    jnp.zeros((1, 1), F32))
        c0 = c0_ref[0, d, e]
        z = jnp.zeros_like(c0)
        sel_m = (m_rows == d) & (m_cols == 2 * p + e)
        m0 = jnp.sum(jnp.sum(jnp.where(sel_m, m0_ref[0], 0.0), axis=1, keepdims=True),
                     axis=0, keepdims=True)
        return (jnp.concatenate([z, c0] if e else [c0, z], axis=0),
                jnp.where(head_mask(e), n0_ref[0, d, pl.ds(p, 1), :], 0.0), m0)

    def chunk_rows(c):
        r0 = c * L
        return pl.ds(r0 if isinstance(r0, int) else pl.multiple_of(r0, L), L)

    def chunk_step(ci, carry, e, d):
        cst, nrow, m = carry
        c = ci if d == 0 else nc - 1 - ci
        rows = chunk_rows(c)
        hmask = head_mask(e)
        vcols = slice(e * DV_A, (e + 1) * DV_A)
        colidx = 2 * p + e + H_A * d
        causal = (tcol <= trow) if d == 0 else (tcol >= trow)
        qm = jnp.where(hmask, q_ref[rows, :].astype(F32) * (DQK_A ** -0.5), 0.0)
        qmb = qm.astype(BF16)
        kb = k_ref[rows, :]
        vh = v_ref[rows, vcols]
        s = _dot_nt(qmb, kb)
        pick = lane_l == colidx
        bcol = jnp.sum(jnp.where(pick, bc_scr[rows, :], 0.0), axis=1, keepdims=True)
        acol = jnp.sum(jnp.where(pick, a_scr[rows, :], 0.0), axis=1, keepdims=True)
        arow = at_scr[c, pl.ds(colidx, 1), :]
        logd = jnp.where(causal, bcol + arow, NEG_BIG)
        m_inter = bcol + m
        m_t = jnp.maximum(m_inter, jnp.max(logd, axis=1, keepdims=True))
        pmat = s * jnp.exp(logd - m_t)
        inter = jnp.exp(m_inter - m_t)
        num = _dot(pmat.astype(BF16), vh) + inter * _dot(qmb, cst.astype(BF16))
        den = (jnp.sum(pmat, axis=1, keepdims=True)
               + inter * jnp.sum(qm * nrow, axis=1, keepdims=True))
        hacc_scr[d, e, rows, :] = num / jnp.maximum(jnp.abs(den), jnp.exp(-m_t))
        b_last = bcol[L - 1:L, :] if d == 0 else bcol[0:1, :]
        logw = b_last + acol
        m_new = jnp.maximum(b_last + m, jnp.max(logw, axis=0, keepdims=True))
        wk = jnp.exp(logw - m_new) * jnp.where(hmask, kb.astype(F32), 0.0)
        decay = jnp.exp(b_last + m - m_new)
        cst = decay * cst + _dot_tn(wk.astype(BF16), vh)
        nrow = decay * nrow + jnp.sum(wk, axis=0, keepdims=True)
        return cst, nrow, m_new

    def all_chains(ci, carry):
        return tuple(chunk_step(ci, st, e, d) for (e, d), st in zip(chains, carry))

    def finish(c, _):
        rows = chunk_rows(c)
        for e in (0, 1):
            vcols = slice(e * DV_A, (e + 1) * DV_A)
            ht = hacc_scr[0, e, rows, :] + hacc_scr[1, e, rows, :]
            ms = jnp.mean(ht * ht, axis=1, keepdims=True)
            hn = ht * lax.rsqrt(ms + EPS) * nrm_ref[:, vcols]
            og = o_ref[rows, vcols].astype(F32)
            y_ref[rows, vcols] = (hn * _sigmoid(og)).astype(BF16)
        return 0

    init = tuple(initial_state(e, d) for e, d in chains)
    if nc == 1:
        final = all_chains(0, init)
        finish(0, 0)
    else:
        final = lax.fori_loop(0, nc, all_chains, init)
        lax.fori_loop(0, nc, finish, 0)

    if emit_state:
        state = dict(zip(chains, final))
        for (e, d), (cst, _, m) in state.items():
            co_ref[0, d, e] = cst[e * DQK_A:(e + 1) * DQK_A, :]
            sel_m = (m_rows == d) & (m_cols == 2 * p + e)
            mo_ref[0] = jnp.where(sel_m, m, mo_ref[0])
        for d in (0, 1):
            no_ref[0, d, pl.ds(p, 1), :] = state[(0, d)][1] + state[(1, d)][1]


def _mlstm(proj, col0, gates, b_gates2, mlstm_norm, batch, seq, chunk, state=None,
           emit_state=False):
    npairs = H_A // 2
    qblk = col0 // LANES
    kblk = qblk + (H_A * DQK_A) // LANES
    vblk = (col0 + 2 * H_A * DQK_A) // (2 * DV_A)
    oblk = vblk + (H_A * DV_A) // (2 * DV_A)
    ng = gates.shape[1]
    in_specs = [pl.BlockSpec((seq, LANES), lambda b, p: (b, qblk + p)),
                pl.BlockSpec((seq, LANES), lambda b, p: (b, kblk + p)),
                pl.BlockSpec((seq, 2 * DV_A), lambda b, p: (b, vblk + p)),
                pl.BlockSpec((seq, 2 * DV_A), lambda b, p: (b, oblk + p)),
                pl.BlockSpec((seq, ng), lambda b, p: (b, 0)),
                pl.BlockSpec((1, ng), lambda b, p: (0, 0)),
                pl.BlockSpec((1, 2 * DV_A), lambda b, p: (0, p))]
    args = [proj, proj, proj, proj, gates, b_gates2, mlstm_norm]
    if state is not None:
        c0, n0, m0 = state
        in_specs += [pl.BlockSpec((1, 2, 2, DQK_A, DV_A), lambda b, p: (b, 0, p, 0, 0)),
                     pl.BlockSpec((1, 2, npairs, LANES), lambda b, p: (b, 0, 0, 0)),
                     pl.BlockSpec((1, 2, H_A), lambda b, p: (b, 0, 0))]
        args += [c0, n0, m0]
    out_specs = [pl.BlockSpec((seq, 2 * DV_A), lambda b, p: (b, p))]
    out_shape = [jax.ShapeDtypeStruct((batch * seq, H_A * DV_A), BF16)]
    if emit_state:
        out_specs += [pl.BlockSpec((1, 2, 2, DQK_A, DV_A), lambda b, p: (b, 0, p, 0, 0)),
                      pl.BlockSpec((1, 2, npairs, LANES), lambda b, p: (b, 0, 0, 0)),
                      pl.BlockSpec((1, 2, H_A), lambda b, p: (b, 0, 0))]
        out_shape += [jax.ShapeDtypeStruct((batch, 2, H_A, DQK_A, DV_A), F32),
                      jax.ShapeDtypeStruct((batch, 2, npairs, LANES), F32),
                      jax.ShapeDtypeStruct((batch, 2, H_A), F32)]
    return pl.pallas_call(
        functools.partial(_mlstm_kernel, seq=seq, chunk=chunk, has_state=state is not None,
                          emit_state=emit_state),
        grid=(batch, npairs),
        in_specs=in_specs,
        out_specs=out_specs,
        out_shape=out_shape,
        scratch_shapes=[pltpu.VMEM((seq, LANES), F32), pltpu.VMEM((seq, LANES), F32),
                        pltpu.VMEM((seq // chunk, LANES, chunk), F32),
                        pltpu.VMEM((2, 2, seq, DV_A), F32)],
        compiler_params=_params("arbitrary", "arbitrary"),
        name="mlstm",
    )(*args)


def _rope(x, cos, sin_signed, lane):
    ahead = pltpu.roll(x, LANES - DH_B // 4, axis=1)
    behind = pltpu.roll(x, DH_B // 4, axis=1)
    even_quarter = (lane // (DH_B // 4)) % 2 == 0
    return x * cos + jnp.where(even_quarter, ahead, behind) * sin_signed


def _ones_column(rows):
    lane = lax.broadcasted_iota(jnp.int32, (rows, LANES), 1)
    return jnp.where(lane == 0, 1.0, 0.0).astype(BF16)


def _lambda(lam_ref, lambda_init):
    lp = lam_ref[...]
    return (jnp.exp(jnp.sum(lp[0:1] * lp[1:2], axis=1, keepdims=True))
            - jnp.exp(jnp.sum(lp[2:3] * lp[3:4], axis=1, keepdims=True)) + lambda_init)


def _diff_attend(q, keys, v_aug, lam, dn, lambda_init):
    lane = lax.broadcasted_iota(jnp.int32, (1, LANES), 1)
    q = q * (DH_B ** -0.5 * LOG2_E)

    def softmax_pv(s):
        ex = jnp.exp2(s - jnp.max(s, axis=1, keepdims=True)).astype(BF16)
        r = _dot(ex, v_aug)
        return r[:, :DV_B] * (1.0 / r[:, DV_B:DV_B + 1])

    s1 = _dot_nt(jnp.where(lane < DH_B, q, 0.0).astype(BF16), keys)
    s2 = _dot_nt(jnp.where(lane >= DH_B, q, 0.0).astype(BF16), keys)
    o = softmax_pv(s1) - lam * softmax_pv(s2)
    ms = jnp.mean(o * o, axis=1, keepdims=True)
    return (o * lax.rsqrt(ms + EPS) * dn) * (1.0 - lambda_init)


def _attn_latent_kernel(q_ref, k_ref, v_ref, ck_ref, cv_ref, cos_ref, sin_ref, lam_ref, dn_ref,
                        o_ref, k_scr, v_scr, *, tq, sub, seq, lambda_init):
    qi = pl.program_id(2)
    lane = lax.broadcasted_iota(jnp.int32, (1, LANES), 1)

    @pl.when(qi == 0)
    def _():
        k_scr[0:seq, :] = _rope(k_ref[...], cos_ref[...], sin_ref[...], lane).astype(BF16)
        k_scr[seq:, :] = ck_ref[0, 0].astype(BF16)
        v_scr[0:seq, 0:DV_B] = v_ref[...].astype(BF16)
        v_scr[seq:, 0:DV_B] = cv_ref[0, 0].astype(BF16)
        v_scr[:, DV_B:] = _ones_column(v_scr.shape[0])

    lam = _lambda(lam_ref, lambda_init)
    for r0 in range(0, tq, sub):
        rows = pl.ds(pl.multiple_of(qi * tq + r0, sub), sub)
        q = _rope(q_ref[r0:r0 + sub, :], cos_ref[rows, :], sin_ref[rows, :], lane)
        o = _diff_attend(q, k_scr[...], v_scr[...], lam, dn_ref[...], lambda_init)
        o_ref[r0:r0 + sub, :] = o.astype(BF16)


def _attn_context_kernel(q_ref, k_ref, v_ref, lam_ref, dn_ref, o_ref, nk_ref, nv_ref, *,
                         lambda_init):
    lam = _lambda(lam_ref, lambda_init)
    ones = _ones_column(q_ref.shape[0])
    for h in range(H_B):
        cols = slice(h * LANES, (h + 1) * LANES)
        k = k_ref[:, cols]
        v = v_ref[:, cols]
        nk_ref[0, h] = k
        nv_ref[0, h] = v
        v_aug = jnp.concatenate([v.astype(BF16), ones], axis=1)
        o = _diff_attend(q_ref[:, cols], k.astype(BF16), v_aug, lam, dn_ref[...], lambda_init)
        o_ref[:, cols] = o.astype(BF16)


def _attention_context(qkv, lam_params, diff_norm, batch, seq, lambda_init):
    w = H_B * LANES
    return pl.pallas_call(
        functools.partial(_attn_context_kernel, lambda_init=lambda_init),
        grid=(batch,),
        in_specs=[pl.BlockSpec((seq, w), lambda b: (b, 0)),
                  pl.BlockSpec((seq, w), lambda b: (b, 1)),
                  pl.BlockSpec((seq, w), lambda b: (b, 2)),
                  pl.BlockSpec(lam_params.shape, lambda b: (0, 0)),
                  pl.BlockSpec((1, DV_B), lambda b: (0, 0))],
        out_specs=[pl.BlockSpec((seq, w), lambda b: (b, 0)),
                   pl.BlockSpec((1, H_B, seq, LANES), lambda b: (b, 0, 0, 0)),
                   pl.BlockSpec((1, H_B, seq, DV_B), lambda b: (b, 0, 0, 0))],
        out_shape=[jax.ShapeDtypeStruct((batch * seq, w), BF16),
                   jax.ShapeDtypeStruct((batch, H_B, seq, LANES), F32),
                   jax.ShapeDtypeStruct((batch, H_B, seq, DV_B), F32)],
        compiler_params=_params("arbitrary"),
        name="attention_context",
    )(qkv, qkv, qkv, lam_params, diff_norm)


def _attention_latent(qkv, lam_params, diff_norm, batch, seq, tq, lambda_init, cache, rope_tables):
    ck, cv = cache
    past = ck.shape[2]
    cos, sin_signed = rope_tables
    nq = seq // tq
    return pl.pallas_call(
        functools.partial(_attn_latent_kernel, tq=tq, sub=min(256, tq), seq=seq,
                          lambda_init=lambda_init),
        grid=(batch, H_B, nq),
        in_specs=[pl.BlockSpec((tq, LANES), lambda b, h, i: (b * nq + i, h)),
                  pl.BlockSpec((seq, LANES), lambda b, h, i: (b, H_B + h)),
                  pl.BlockSpec((seq, LANES), lambda b, h, i: (b, 2 * H_B + h)),
                  pl.BlockSpec((1, 1, past, LANES), lambda b, h, i: (b, h, 0, 0)),
                  pl.BlockSpec((1, 1, past, LANES), lambda b, h, i: (b, h, 0, 0)),
                  pl.BlockSpec((seq, LANES), lambda b, h, i: (0, 0)),
                  pl.BlockSpec((seq, LANES), lambda b, h, i: (0, 0)),
                  pl.BlockSpec(lam_params.shape, lambda b, h, i: (0, 0)),
                  pl.BlockSpec((1, DV_B), lambda b, h, i: (0, 0))],
        out_specs=pl.BlockSpec((tq, DV_B), lambda b, h, i: (b * nq + i, h)),
        out_shape=jax.ShapeDtypeStruct((batch * seq, H_B * DV_B), BF16),
        scratch_shapes=[pltpu.VMEM((seq + past, LANES), BF16),
                        pltpu.VMEM((seq + past, 2 * DV_B), BF16)],
        compiler_params=_params("arbitrary", "arbitrary", "arbitrary"),
        name="attention_latent",
    )(qkv, qkv, qkv, ck, cv, cos, sin_signed, lam_params, diff_norm)


def _merge_kernel(ya_ref, yb_ref, ga_ref, gb_ref, x_ref, mod_ref, wpa_ref, wpb_ref, wo_ref, o_ref):
    ya = _dot(ya_ref[...], wpa_ref[...])
    yb = _dot(yb_ref[...], wpb_ref[...])
    y = _sigmoid(ga_ref[...].astype(F32)) * ya + _sigmoid(gb_ref[...].astype(F32)) * yb
    z = _dot(y.astype(BF16), wo_ref[...])
    o_ref[...] = x_ref[...] + mod_ref[0, 2:3, :] * z


def _merge_project(y_a, y_b, proj, x2d, mod3, mod_row, w_pa, w_pb, w_out, tm=512):
    t, d = x2d.shape
    wa = y_a.shape[1]
    wb = y_b.shape[1]
    const = dict(pipeline_mode=pl.Buffered(1))
    return pl.pallas_call(
        _merge_kernel,
        grid=(t // tm,),
        in_specs=[pl.BlockSpec((tm, wa), lambda i: (i, 0)),
                  pl.BlockSpec((tm, wb), lambda i: (i, 0)),
                  pl.BlockSpec((tm, d), lambda i: (i, 0)),
                  pl.BlockSpec((tm, d), lambda i: (i, 1)),
                  pl.BlockSpec((tm, d), lambda i: (i, 0)),
                  pl.BlockSpec((1,) + mod3.shape[1:], lambda i: (mod_row(i * tm), 0, 0)),
                  pl.BlockSpec((wa, d), lambda i: (0, 0), **const),
                  pl.BlockSpec((wb, d), lambda i: (0, 0), **const),
                  pl.BlockSpec((d, d), lambda i: (0, 0), **const)],
        out_specs=pl.BlockSpec((tm, d), lambda i: (i, 0)),
        out_shape=jax.ShapeDtypeStruct((t, d), F32),
        compiler_params=_params("arbitrary"),
        name="merge_project",
    )(y_a, y_b, proj, proj, x2d, mod3, w_pa, w_pb, w_out)


def _ffn_kernel(x_ref, xp_ref, xn_ref, mod_ref, n2_ref, wa_ref, wb_ref, cwa_ref, cwb_ref,
                cba_ref, cbb_ref, wd_ref, fn_ref, o_ref, h_scr, acc_scr, ua_scr, ub_scr,
                *, tm, seq):
    i = pl.program_id(0)
    j = pl.program_id(1)
    halo = BF16_ROWS

    @pl.when(j == 0)
    def _():
        def nrm(x):
            return _modulated_norm(x, n2_ref[...], mod_ref[0, 4:5, :], mod_ref[0, 3:4, :])
        starts_seq = (i * tm) % seq == 0
        ends_seq = ((i + 1) * tm) % seq == 0
        h_scr[0:halo, :] = jnp.where(starts_seq, 0.0, nrm(xp_ref[...])).astype(BF16)
        h_scr[halo:halo + tm, :] = nrm(x_ref[...]).astype(BF16)
        h_scr[halo + tm:, :] = jnp.where(ends_seq, 0.0, nrm(xn_ref[...])).astype(BF16)
        acc_scr[...] = jnp.zeros(acc_scr.shape, F32)

    h = h_scr[...]
    ua_scr[...] = _dot(h, wa_ref[...])
    ub_scr[...] = _dot(h, wb_ref[...])

    inner = range(seq, tm, seq)
    sub = lax.broadcasted_iota(jnp.int32, (8, 1), 0)

    def window(u_scr, shift, drop_rows, drop_sub):
        parts = []
        r0 = 0
        for r in drop_rows:
            slab = r // 8 * 8
            if slab > r0:
                parts.append(u_scr[halo + shift + r0:halo + shift + slab, :])
            parts.append(jnp.where(sub == drop_sub, 0.0,
                                   u_scr[halo + shift + slab:halo + shift + slab + 8, :]))
            r0 = slab + 8
        parts.append(u_scr[halo + shift + r0:halo + shift + tm, :])
        return parts[0] if len(parts) == 1 else jnp.concatenate(parts, axis=0)

    def conv(u_scr, cw_ref, cb_ref):
        prev = window(u_scr, -1, list(inner), 0)
        cur = u_scr[halo:halo + tm, :]
        nxt = window(u_scr, 1, [r - 1 for r in inner], 7)
        return prev * cw_ref[0:1, :] + cur * cw_ref[1:2, :] + nxt * cw_ref[2:3, :] + cb_ref[...]

    a = conv(ua_scr, cwa_ref, cba_ref)
    b = conv(ub_scr, cwb_ref, cbb_ref)
    g = (a * _sigmoid(a)) * b
    acc_scr[...] += _dot(g.astype(BF16), wd_ref[...])

    @pl.when(j == pl.num_programs(1) - 1)
    def _():
        x2 = x_ref[...] + mod_ref[0, 5:6, :] * acc_scr[...]
        ms = jnp.mean(x2 * x2, axis=-1, keepdims=True)
        o_ref[...] = x2 * lax.rsqrt(ms + EPS) * fn_ref[...]


def _conv_ffn(x2d, mod3, mod_row, norm2, w_up, conv_w, conv_b, w_down, final_norm, seq,
              tm=512, tf=512):
    t, d = x2d.shape
    dff = w_down.shape[0]
    nf = dff // tf
    halo = BF16_ROWS
    nhalo = t // halo
    per = tm // halo
    assert tm % seq == 0 or seq % tm == 0, "sequence boundaries must sit at static rows of a tile"
    return pl.pallas_call(
        functools.partial(_ffn_kernel, tm=tm, seq=seq),
        grid=(t // tm, nf),
        in_specs=[pl.BlockSpec((tm, d), lambda i, j: (i, 0)),
                  pl.BlockSpec((halo, d), lambda i, j: (jnp.maximum(i * per - 1, 0), 0)),
                  pl.BlockSpec((halo, d), lambda i, j: (jnp.minimum((i + 1) * per, nhalo - 1), 0)),
                  pl.BlockSpec((1,) + mod3.shape[1:], lambda i, j: (mod_row(i * tm), 0, 0)),
                  pl.BlockSpec((1, d), lambda i, j: (0, 0)),
                  pl.BlockSpec((d, tf), lambda i, j: (0, j)),
                  pl.BlockSpec((d, tf), lambda i, j: (0, nf + j)),
                  pl.BlockSpec((3, tf), lambda i, j: (0, j)),
                  pl.BlockSpec((3, tf), lambda i, j: (0, nf + j)),
                  pl.BlockSpec((1, tf), lambda i, j: (0, j)),
                  pl.BlockSpec((1, tf), lambda i, j: (0, nf + j)),
                  pl.BlockSpec((tf, d), lambda i, j: (j, 0)),
                  pl.BlockSpec((1, d), lambda i, j: (0, 0))],
        out_specs=pl.BlockSpec((tm, d), lambda i, j: (i, 0)),
        out_shape=jax.ShapeDtypeStruct((t, d), F32),
        scratch_shapes=[pltpu.VMEM((tm + 2 * halo, d), BF16), pltpu.VMEM((tm, d), F32),
                        pltpu.VMEM((tm + 2 * halo, tf), F32), pltpu.VMEM((tm + 2 * halo, tf), F32)],
        compiler_params=_params("arbitrary", "arbitrary"),
        name="conv_ffn",
    )(x2d, x2d, x2d, mod3, norm2, w_up, w_up, conv_w, conv_w, conv_b, conv_b, w_down, final_norm)


def _rope_tables(n_tokens):
    rows = n_tokens // GRID_W
    row = jnp.repeat(jnp.arange(rows), GRID_W).astype(F32)
    col = jnp.tile(jnp.arange(GRID_W), rows).astype(F32)
    quarter = DH_B // 4
    inv_freq = jnp.power(ROPE_BASE, -jnp.arange(quarter, dtype=F32) / quarter)
    ang_r = row[:, None] * inv_freq
    ang_c = col[:, None] * inv_freq
    ang = jnp.concatenate([ang_r, ang_r, ang_c, ang_c] * 2, axis=-1)
    sign = jnp.where((jnp.arange(2 * DH_B) // quarter) % 2 == 0, -1.0, 1.0).astype(F32)
    return jnp.cos(ang), jnp.sin(ang) * sign


def _gate_columns(g):
    lead = g.shape[:-1]
    g4 = g.reshape(lead + (4, H_A))
    pad = jnp.zeros(lead + (LANES - 2 * H_A,), g.dtype)
    return jnp.concatenate([g4[..., 0, :], g4[..., 2, :], pad, g4[..., 1, :], g4[..., 3, :], pad],
                           axis=-1)


def _trunk(x2d, batch, seq, mod3, mod_row, wts, lambda_init, chunk, cache=None, state=None,
           rope_tables=None, emit_state=False):
    t, d = x2d.shape
    span = t if cache is None else seq
    proj, qkv, gates = _in_projection(x2d, mod3, mod_row, wts["norm1"], wts["w_full"],
                                      wts["w_tail"], wts["w_gates"], wts["cols_a"], wts["cols_b"],
                                      tm=min(1024, span))
    a_out = _mlstm(proj, 2 * d, gates, wts["b_gates"], wts["mlstm_norm"], batch, seq, chunk,
                   state=state, emit_state=emit_state)
    if cache is None:
        b_out = _attention_context(qkv, wts["lam"], wts["diff_norm"], batch, seq, lambda_init)
    else:
        b_out = [_attention_latent(qkv, wts["lam"], wts["diff_norm"], batch, seq, min(1024, seq), lambda_init,
                                   cache, rope_tables)]
    x1 = _merge_project(a_out[0], b_out[0], proj, x2d, mod3, mod_row, wts["w_pa"], wts["w_pb"],
                        wts["w_out"], tm=min(512, span))
    y = _conv_ffn(x1, mod3, mod_row, wts["norm2"], wts["w_up"], wts["conv_w"], wts["conv_b"],
                  wts["w_down"], wts["final_norm"], seq, tm=min(512, span))
    return y, a_out[1:], b_out[1:]


def kernel(x_prompt, x_sample, cache_k, cache_v, state_C, state_n, state_m, c, c_ctx, w_mod, b_mod, norm1, w_in, b_gates, mlstm_norm, lam_q1, lam_k1, lam_q2, lam_k2, diff_norm, w_pa, w_pb, w_out, norm2, w_up, conv_w, conv_b, w_down, final_norm):
    assert w_in.shape[0] == 1, "single trunk layer"
    bp, sp, d = x_prompt.shape
    bs, ss, _ = x_sample.shape
    past = cache_k.shape[3]
    lambda_init = 0.8 - 0.6 * math.exp(-0.3 * 0)

    wa_cols = 2 * H_A * DQK_A + 2 * H_A * DV_A
    wb_cols = 3 * H_B * 2 * DH_B
    g0 = wa_cols
    b0 = g0 + 4 * H_A
    w = w_in.reshape(w_in.shape[1:])
    w16 = w.astype(BF16)
    wts = {
        "w_full": w16, "w_tail": w16[:, b0:], "cols_a": wa_cols, "cols_b": wb_cols,
        "w_gates": _gate_columns(w[:, g0:b0]).astype(BF16),
        "b_gates": _gate_columns(b_gates.reshape(-1))[None, :],
        "norm1": norm1, "norm2": norm2, "mlstm_norm": mlstm_norm, "diff_norm": diff_norm,
        "lam": jnp.concatenate([lam_q1, lam_k1, lam_q2, lam_k2], axis=0),
        "w_pa": w_pa.reshape(w_pa.shape[1:]).astype(BF16),
        "w_pb": w_pb.reshape(w_pb.shape[1:]).astype(BF16),
        "w_out": w_out.reshape(w_out.shape[1:]).astype(BF16),
        "w_up": w_up.reshape(w_up.shape[1:]).astype(BF16),
        "conv_w": conv_w.reshape(conv_w.shape[1:]), "conv_b": conv_b,
        "w_down": w_down.reshape(w_down.shape[1:]).astype(BF16),
        "final_norm": final_norm[None, :],
    }

    c_all = jnp.concatenate([c_ctx[None, :], c, jnp.zeros((8 - 1 - bs, d), F32)], axis=0)
    mod3 = _modulation(c_all, w_mod.reshape(w_mod.shape[1:]), b_mod).reshape(8, 6, d)

    yp, (c_new, n_new, m_new), (k_new, v_new) = _trunk(
        x_prompt.reshape(bp * sp, d), bp, sp, mod3, lambda t: 0, wts, lambda_init,
        chunk=sp, emit_state=True)

    state = (state_C.reshape(bs, 2, H_A, DQK_A, DV_A), state_n.reshape(bs, 2, H_A // 2, LANES),
             state_m.reshape(bs, 2, H_A))
    cache = (cache_k.reshape(bs, H_B, past, 2 * DH_B), cache_v.reshape(bs, H_B, past, DV_B))
    ys, _, _ = _trunk(
        x_sample.reshape(bs * ss, d), bs, ss, mod3, lambda t: 1 + t // ss, wts, lambda_init,
        chunk=256, cache=cache, state=state, rope_tables=_rope_tables(ss))

    return (yp.reshape(bp, sp, d), ys.reshape(bs, ss, d),
            k_new.reshape(bp, 1, H_B, sp, 2 * DH_B), v_new.reshape(bp, 1, H_B, sp, DV_B),
            c_new.reshape(bp, 1, 2, H_A, DQK_A, DV_A), n_new.reshape(bp, 1, 2, H_A, DQK_A),
            m_new.reshape(bp, 1, 2, H_A))
```

```python
import functools
import math

import jax
import jax.numpy as jnp
from jax import lax
from jax.experimental import pallas as pl
from jax.experimental.pallas import tpu as pltpu

F32 = jnp.float32
BF16 = jnp.bfloat16

GRID_W = 64
H_A = 8
DQK_A = 64
DV_A = 128
H_B = 8
DH_B = 64
DV_B = 128
ROPE_BASE = 10000.0
GATE_CAP = 15.0
EPS = 1e-6
NEG_BIG = -1e30
LOG2_E = 1.4426950408889634

LANES = 128
BF16_ROWS = 16
MXU_COLS = 256
VMEM_LIMIT = 56 * 1024 * 1024


def _sigmoid(x):
    return 1.0 / (1.0 + jnp.exp(-x))


def _dot(a, b):
    return jnp.dot(a, b, preferred_element_type=F32)


def _dot_nt(a, b):
    return lax.dot_general(a, b, (((1,), (1,)), ((), ())), preferred_element_type=F32)


def _dot_tn(a, b):
    return lax.dot_general(a, b, (((0,), (0,)), ((), ())), preferred_element_type=F32)


def _tile(n, want):
    t = min(want, n)
    while n % t:
        t -= LANES
    return t


def _params(*sem):
    return pltpu.CompilerParams(dimension_semantics=sem, vmem_limit_bytes=VMEM_LIMIT)


def _mod_kernel(c_ref, w_ref, b_ref, o_ref):
    c = c_ref[...]
    s = (c * _sigmoid(c)).astype(BF16)
    o_ref[...] = _dot(s, w_ref[...].astype(BF16)) + b_ref[...]


def _modulation(c_all, w_mod, b_mod, tn=1024):
    rows, d = c_all.shape
    n = w_mod.shape[1]
    tn = _tile(n, tn)
    return pl.pallas_call(
        _mod_kernel,
        grid=(n // tn,),
        in_specs=[pl.BlockSpec((rows, d), lambda j: (0, 0)),
                  pl.BlockSpec((d, tn), lambda j: (0, j)),
                  pl.BlockSpec((1, tn), lambda j: (0, j))],
        out_specs=pl.BlockSpec((rows, tn), lambda j: (0, j)),
        out_shape=jax.ShapeDtypeStruct((rows, n), F32),
        compiler_params=_params("arbitrary"),
        name="modulation",
    )(c_all, w_mod, b_mod)


def _modulated_norm(x, w, scale, shift):
    ms = jnp.mean(x * x, axis=-1, keepdims=True)
    return (x * lax.rsqrt(ms + EPS) * w) * (1.0 + scale) + shift


def _inproj_kernel(x_ref, mod_ref, n1_ref, w_ref, wg_ref, o16_ref, o32_ref, g_ref, h_scr, *, n16):
    j = pl.program_id(1)

    @pl.when(j == 0)
    def _():
        h = _modulated_norm(x_ref[...], n1_ref[...], mod_ref[0, 1:2, :], mod_ref[0, 0:1, :])
        hb = h.astype(BF16)
        h_scr[...] = hb
        g_ref[...] = _dot(hb, wg_ref[...])

    @pl.when(j < n16)
    def _():
        o16_ref[...] = _dot(h_scr[...], w_ref[...]).astype(BF16)

    @pl.when(j >= n16)
    def _():
        o32_ref[...] = _dot(h_scr[...], w_ref[...])


def _in_projection(x2d, mod3, mod_row, norm1, w_main, w_gates, cols16, tm=1024, tn=1024):
    t, d = x2d.shape
    n = w_main.shape[1]
    ng = w_gates.shape[1]
    tn = _tile(math.gcd(cols16, n - cols16), tn)
    n16 = cols16 // tn
    n32 = (n - cols16) // tn
    return pl.pallas_call(
        functools.partial(_inproj_kernel, n16=n16),
        grid=(t // tm, n16 + n32),
        in_specs=[pl.BlockSpec((tm, d), lambda i, j: (i, 0)),
                  pl.BlockSpec((1,) + mod3.shape[1:], lambda i, j: (mod_row(i * tm), 0, 0)),
                  pl.BlockSpec((1, d), lambda i, j: (0, 0)),
                  pl.BlockSpec((d, tn), lambda i, j: (0, j)),
                  pl.BlockSpec((d, ng), lambda i, j: (0, 0))],
        out_specs=[pl.BlockSpec((tm, tn), lambda i, j: (i, jnp.minimum(j, n16 - 1))),
                   pl.BlockSpec((tm, tn), lambda i, j: (i, jnp.maximum(j - n16, 0))),
                   pl.BlockSpec((tm, ng), lambda i, j: (i, 0))],
        out_shape=[jax.ShapeDtypeStruct((t, cols16), BF16),
                   jax.ShapeDtypeStruct((t, n - cols16), F32),
                   jax.ShapeDtypeStruct((t, ng), F32)],
        scratch_shapes=[pltpu.VMEM((tm, d), BF16)],
        compiler_params=_params("arbitrary", "arbitrary"),
        name="in_projection",
    )(x2d, mod3, norm1, w_main, w_gates)


def _split3(x):
    hi = x.astype(BF16)
    r = x - hi.astype(F32)
    mid = r.astype(BF16)
    lo = (r - mid.astype(F32)).astype(BF16)
    return hi, mid, lo


def _mlstm_kernel(*refs, seq, chunk, has_state, emit_state):
    q_ref, k_ref, v_ref, o_ref, g_ref, bg_ref, nrm_ref = refs[:7]
    pos = 7
    if has_state:
        c0_ref, n0_ref, m0_ref = refs[pos:pos + 3]
        pos += 3
    y_ref = refs[pos]
    pos += 1
    if emit_state:
        co_ref, no_ref, mo_ref = refs[pos:pos + 3]
        pos += 3
    bc_scr, a_scr, at_scr, hacc_scr = refs[pos:pos + 4]

    L = chunk
    nc = seq // L
    p = pl.program_id(1)

    @pl.when(p == 0)
    def _():
        row = lax.broadcasted_iota(jnp.int32, (L, L), 0)
        col = lax.broadcasted_iota(jnp.int32, (L, L), 1)
        tril = jnp.where(col <= row, 1.0, 0.0).astype(BF16)
        triu = jnp.where(col >= row, 1.0, 0.0).astype(BF16)
        lane = lax.broadcasted_iota(jnp.int32, (L, LANES), 1)
        for c in range(nc):
            rows = pl.ds(c * L, L)
            g = g_ref[rows, :] + bg_ref[...]
            g = GATE_CAP * jnp.tanh(g / GATE_CAP)
            gi = g[:, :LANES]
            gf = g[:, LANES:]
            lf = jnp.minimum(gf, 0.0) - jnp.log(1.0 + jnp.exp(-jnp.abs(gf)))
            hi, mid, lo = _split3(lf)
            pre = _dot(tril, hi) + _dot(tril, mid) + _dot(tril, lo)
            suf = _dot(triu, hi) + _dot(triu, mid) + _dot(triu, lo)
            bc = jnp.where(lane < H_A, pre, suf)
            a = gi - bc
            bc_scr[rows, :] = bc
            a_scr[rows, :] = a
            at_scr[c] = a.T
        if emit_state:
            mo_ref[...] = jnp.zeros(mo_ref.shape, F32)

    lane_row = lax.broadcasted_iota(jnp.int32, (1, LANES), 1)
    lane_l = lax.broadcasted_iota(jnp.int32, (L, LANES), 1)
    trow = lax.broadcasted_iota(jnp.int32, (L, L), 0)
    tcol = lax.broadcasted_iota(jnp.int32, (L, L), 1)
    m_rows = lax.broadcasted_iota(jnp.int32, (2, H_A), 0)
    m_cols = lax.broadcasted_iota(jnp.int32, (2, H_A), 1)

    chains = [(e, d) for e in (0, 1) for d in (0, 1)]

    def head_mask(e):
        return (lane_row >= DQK_A) if e else (lane_row < DQK_A)

    def initial_state(e, d):
        if not has_state:
            return (jnp.zeros((2 * DQK_A, DV_A), F32), jnp.zeros((1, LANES), F32),
                    jnp.zeros((1, 1), F32))
        c0 = c0_ref[0, d, e]
        z = jnp.zeros_like(c0)
        sel_m = (m_rows == d) & (m_cols == 2 * p + e)
        m0 = jnp.sum(jnp.sum(jnp.where(sel_m, m0_ref[0], 0.0), axis=1, keepdims=True),
                     axis=0, keepdims=True)
        return (jnp.concatenate([z, c0] if e else [c0, z], axis=0),
                jnp.where(head_mask(e), n0_ref[0, d, pl.ds(p, 1), :], 0.0), m0)

    def chunk_rows(c):
        r0 = c * L
        return pl.ds(r0 if isinstance(r0, int) else pl.multiple_of(r0, L), L)

    def chunk_step(ci, carry, e, d):
        cst, nrow, m = carry
        c = ci if d == 0 else nc - 1 - ci
        rows = chunk_rows(c)
        hmask = head_mask(e)
        vcols = slice(e * DV_A, (e + 1) * DV_A)
        colidx = 2 * p + e + H_A * d
        causal = (tcol <= trow) if d == 0 else (tcol >= trow)
        qm = jnp.where(hmask, q_ref[rows, :].astype(F32) * (DQK_A ** -0.5), 0.0)
        qmb = qm.astype(BF16)
        kb = k_ref[rows, :]
        vh = v_ref[rows, vcols]
        s = _dot_nt(qmb, kb)
        pick = lane_l == colidx
        bcol = jnp.sum(jnp.where(pick, bc_scr[rows, :], 0.0), axis=1, keepdims=True)
        acol = jnp.sum(jnp.where(pick, a_scr[rows, :], 0.0), axis=1, keepdims=True)
        arow = at_scr[c, pl.ds(colidx, 1), :]
        logd = jnp.where(causal, bcol + arow, NEG_BIG)
        m_inter = bcol + m
        m_t = jnp.maximum(m_inter, jnp.max(logd, axis=1, keepdims=True))
        pmat = s * jnp.exp(logd - m_t)
        inter = jnp.exp(m_inter - m_t)
        num = _dot(pmat.astype(BF16), vh) + inter * _dot(qmb, cst.astype(BF16))
        den = (jnp.sum(pmat, axis=1, keepdims=True)
               + inter * jnp.sum(qm * nrow, axis=1, keepdims=True))
        hacc_scr[d, e, rows, :] = num / jnp.maximum(jnp.abs(den), jnp.exp(-m_t))
        b_last = bcol[L - 1:L, :] if d == 0 else bcol[0:1, :]
        logw = b_last + acol
        m_new = jnp.maximum(b_last + m, jnp.max(logw, axis=0, keepdims=True))
        wk = jnp.exp(logw - m_new) * jnp.where(hmask, kb.astype(F32), 0.0)
        decay = jnp.exp(b_last + m - m_new)
        cst = decay * cst + _dot_tn(wk.astype(BF16), vh)
        nrow = decay * nrow + jnp.sum(wk, axis=0, keepdims=True)
        return cst, nrow, m_new

    def all_chains(ci, carry):
        return tuple(chunk_step(ci, st, e, d) for (e, d), st in zip(chains, carry))

    def finish(c, _):
        rows = chunk_rows(c)
        for e in (0, 1):
            vcols = slice(e * DV_A, (e + 1) * DV_A)
            ht = hacc_scr[0, e, rows, :] + hacc_scr[1, e, rows, :]
            ms = jnp.mean(ht * ht, axis=1, keepdims=True)
            hn = ht * lax.rsqrt(ms + EPS) * nrm_ref[:, vcols]
            og = o_ref[rows, vcols].astype(F32)
            y_ref[rows, vcols] = (hn * _sigmoid(og)).astype(BF16)
        return 0

    init = tuple(initial_state(e, d) for e, d in chains)
    if nc == 1:
        final = all_chains(0, init)
        finish(0, 0)
    else:
        final = lax.fori_loop(0, nc, all_chains, init)
        lax.fori_loop(0, nc, finish, 0)

    if emit_state:
        state = dict(zip(chains, final))
        for (e, d), (cst, _, m) in state.items():
            co_ref[0, d, e] = cst[e * DQK_A:(e + 1) * DQK_A, :]
            sel_m = (m_rows == d) & (m_cols == 2 * p + e)
            mo_ref[0] = jnp.where(sel_m, m, mo_ref[0])
        for d in (0, 1):
            no_ref[0, d, pl.ds(p, 1), :] = state[(0, d)][1] + state[(1, d)][1]


def _mlstm(proj, col0, gates, b_gates2, mlstm_norm, batch, seq, chunk, state=None,
           emit_state=False):
    npairs = H_A // 2
    qblk = col0 // LANES
    kblk = qblk + (H_A * DQK_A) // LANES
    vblk = (col0 + 2 * H_A * DQK_A) // (2 * DV_A)
    oblk = vblk + (H_A * DV_A) // (2 * DV_A)
    ng = gates.shape[1]
    in_specs = [pl.BlockSpec((seq, LANES), lambda b, p: (b, qblk + p)),
                pl.BlockSpec((seq, LANES), lambda b, p: (b, kblk + p)),
                pl.BlockSpec((seq, 2 * DV_A), lambda b, p: (b, vblk + p)),
                pl.BlockSpec((seq, 2 * DV_A), lambda b, p: (b, oblk + p)),
                pl.BlockSpec((seq, ng), lambda b, p: (b, 0)),
                pl.BlockSpec((1, ng), lambda b, p: (0, 0)),
                pl.BlockSpec((1, 2 * DV_A), lambda b, p: (0, p))]
    args = [proj, proj, proj, proj, gates, b_gates2, mlstm_norm]
    if state is not None:
        c0, n0, m0 = state
        in_specs += [pl.BlockSpec((1, 2, 2, DQK_A, DV_A), lambda b, p: (b, 0, p, 0, 0)),
                     pl.BlockSpec((1, 2, npairs, LANES), lambda b, p: (b, 0, 0, 0)),
                     pl.BlockSpec((1, 2, H_A), lambda b, p: (b, 0, 0))]
        args += [c0, n0, m0]
    out_specs = [pl.BlockSpec((seq, 2 * DV_A), lambda b, p: (b, p))]
    out_shape = [jax.ShapeDtypeStruct((batch * seq, H_A * DV_A), BF16)]
    if emit_state:
        out_specs += [pl.BlockSpec((1, 2, 2, DQK_A, DV_A), lambda b, p: (b, 0, p, 0, 0)),
                      pl.BlockSpec((1, 2, npairs, LANES), lambda b, p: (b, 0, 0, 0)),
                      pl.BlockSpec((1, 2, H_A), lambda b, p: (b, 0, 0))]
        out_shape += [jax.ShapeDtypeStruct((batch, 2, H_A, DQK_A, DV_A), F32),
                      jax.ShapeDtypeStruct((batch, 2, npairs, LANES), F32),
                      jax.ShapeDtypeStruct((batch, 2, H_A), F32)]
    return pl.pallas_call(
        functools.partial(_mlstm_kernel, seq=seq, chunk=chunk, has_state=state is not None,
                          emit_state=emit_state),
        grid=(batch, npairs),
        in_specs=in_specs,
        out_specs=out_specs,
        out_shape=out_shape,
        scratch_shapes=[pltpu.VMEM((seq, LANES), F32), pltpu.VMEM((seq, LANES), F32),
                        pltpu.VMEM((seq // chunk, LANES, chunk), F32),
                        pltpu.VMEM((2, 2, seq, DV_A), F32)],
        compiler_params=_params("arbitrary", "arbitrary"),
        name="mlstm",
    )(*args)


def _rope(x, cos, sin_signed, lane):
    ahead = pltpu.roll(x, LANES - DH_B // 4, axis=1)
    behind = pltpu.roll(x, DH_B // 4, axis=1)
    even_quarter = (lane // (DH_B // 4)) % 2 == 0
    return x * cos + jnp.where(even_quarter, ahead, behind) * sin_signed


def _ones_column(rows):
    lane = lax.broadcasted_iota(jnp.int32, (rows, LANES), 1)
    return jnp.where(lane == 0, 1.0, 0.0).astype(BF16)


def _lambda(lam_ref, lambda_init):
    lp = lam_ref[...]
    return (jnp.exp(jnp.sum(lp[0:1] * lp[1:2], axis=1, keepdims=True))
            - jnp.exp(jnp.sum(lp[2:3] * lp[3:4], axis=1, keepdims=True)) + lambda_init)


def _diff_attend(q, keys, v_aug, lam, dn, lambda_init):
    lane = lax.broadcasted_iota(jnp.int32, (1, LANES), 1)
    q = q * (DH_B ** -0.5 * LOG2_E)

    def softmax_pv(s):
        ex = jnp.exp2(s - jnp.max(s, axis=1, keepdims=True)).astype(BF16)
        r = _dot(ex, v_aug)
        return r[:, :DV_B] * (1.0 / r[:, DV_B:DV_B + 1])

    s1 = _dot_nt(jnp.where(lane < DH_B, q, 0.0).astype(BF16), keys)
    s2 = _dot_nt(jnp.where(lane >= DH_B, q, 0.0).astype(BF16), keys)
    o = softmax_pv(s1) - lam * softmax_pv(s2)
    ms = jnp.mean(o * o, axis=1, keepdims=True)
    return (o * lax.rsqrt(ms + EPS) * dn) * (1.0 - lambda_init)


def _attn_latent_kernel(q_ref, k_ref, v_ref, ck_ref, cv_ref, cos_ref, sin_ref, lam_ref, dn_ref,
                        o_ref, k_scr, v_scr, *, tq, sub, seq, lambda_init):
    qi = pl.program_id(2)
    lane = lax.broadcasted_iota(jnp.int32, (1, LANES), 1)

    @pl.when(qi == 0)
    def _():
        k_scr[0:seq, :] = _rope(k_ref[...], cos_ref[...], sin_ref[...], lane).astype(BF16)
        k_scr[seq:, :] = ck_ref[0, 0].astype(BF16)
        v_scr[0:seq, 0:DV_B] = v_ref[...].astype(BF16)
        v_scr[seq:, 0:DV_B] = cv_ref[0, 0].astype(BF16)
        v_scr[:, DV_B:] = _ones_column(v_scr.shape[0])

    lam = _lambda(lam_ref, lambda_init)
    for r0 in range(0, tq, sub):
        rows = pl.ds(pl.multiple_of(qi * tq + r0, sub), sub)
        q = _rope(q_ref[r0:r0 + sub, :], cos_ref[rows, :], sin_ref[rows, :], lane)
        o = _diff_attend(q, k_scr[...], v_scr[...], lam, dn_ref[...], lambda_init)
        o_ref[r0:r0 + sub, :] = o.astype(BF16)


def _attn_context_kernel(q_ref, k_ref, v_ref, lam_ref, dn_ref, o_ref, nk_ref, nv_ref, *,
                         lambda_init):
    lam = _lambda(lam_ref, lambda_init)
    ones = _ones_column(q_ref.shape[0])
    for h in range(H_B):
        cols = slice(h * LANES, (h + 1) * LANES)
        k = k_ref[:, cols]
        v = v_ref[:, cols]
        nk_ref[0, h] = k
        nv_ref[0, h] = v
        v_aug = jnp.concatenate([v.astype(BF16), ones], axis=1)
        o = _diff_attend(q_ref[:, cols], k.astype(BF16), v_aug, lam, dn_ref[...], lambda_init)
        o_ref[:, cols] = o.astype(BF16)


def _attention_context(qkv, lam_params, diff_norm, batch, seq, lambda_init):
    w = H_B * LANES
    return pl.pallas_call(
        functools.partial(_attn_context_kernel, lambda_init=lambda_init),
        grid=(batch,),
        in_specs=[pl.BlockSpec((seq, w), lambda b: (b, 0)),
                  pl.BlockSpec((seq, w), lambda b: (b, 1)),
                  pl.BlockSpec((seq, w), lambda b: (b, 2)),
                  pl.BlockSpec(lam_params.shape, lambda b: (0, 0)),
                  pl.BlockSpec((1, DV_B), lambda b: (0, 0))],
        out_specs=[pl.BlockSpec((seq, w), lambda b: (b, 0)),
                   pl.BlockSpec((1, H_B, seq, LANES), lambda b: (b, 0, 0, 0)),
                   pl.BlockSpec((1, H_B, seq, DV_B), lambda b: (b, 0, 0, 0))],
        out_shape=[jax.ShapeDtypeStruct((batch * seq, w), BF16),
                   jax.ShapeDtypeStruct((batch, H_B, seq, LANES), F32),
                   jax.ShapeDtypeStruct((batch, H_B, seq, DV_B), F32)],
        compiler_params=_params("arbitrary"),
        name="attention_context",
    )(qkv, qkv, qkv, lam_params, diff_norm)


def _attention_latent(qkv, lam_params, diff_norm, batch, seq, tq, lambda_init, cache, rope_tables):
    ck, cv = cache
    past = ck.shape[2]
    cos, sin_signed = rope_tables
    nq = seq // tq
    return pl.pallas_call(
        functools.partial(_attn_latent_kernel, tq=tq, sub=min(256, tq), seq=seq,
                          lambda_init=lambda_init),
        grid=(batch, H_B, nq),
        in_specs=[pl.BlockSpec((tq, LANES), lambda b, h, i: (b * nq + i, h)),
                  pl.BlockSpec((seq, LANES), lambda b, h, i: (b, H_B + h)),
                  pl.BlockSpec((seq, LANES), lambda b, h, i: (b, 2 * H_B + h)),
                  pl.BlockSpec((1, 1, past, LANES), lambda b, h, i: (b, h, 0, 0)),
                  pl.BlockSpec((1, 1, past, LANES), lambda b, h, i: (b, h, 0, 0)),
                  pl.BlockSpec((seq, LANES), lambda b, h, i: (0, 0)),
                  pl.BlockSpec((seq, LANES), lambda b, h, i: (0, 0)),
                  pl.BlockSpec(lam_params.shape, lambda b, h, i: (0, 0)),
                  pl.BlockSpec((1, DV_B), lambda b, h, i: (0, 0))],
        out_specs=pl.BlockSpec((tq, DV_B), lambda b, h, i: (b * nq + i, h)),
        out_shape=jax.ShapeDtypeStruct((batch * seq, H_B * DV_B), BF16),
        scratch_shapes=[pltpu.VMEM((seq + past, LANES), BF16),
                        pltpu.VMEM((seq + past, 2 * DV_B), BF16)],
        compiler_params=_params("arbitrary", "arbitrary", "arbitrary"),
        name="attention_latent",
    )(qkv, qkv, qkv, ck, cv, cos, sin_signed, lam_params, diff_norm)


def _merge_kernel(ya_ref, yb_ref, ga_ref, gb_ref, x_ref, mod_ref, wpa_ref, wpb_ref, wo_ref, o_ref):
    ya = _dot(ya_ref[...], wpa_ref[...])
    yb = _dot(yb_ref[...], wpb_ref[...])
    y = _sigmoid(ga_ref[...].astype(F32)) * ya + _sigmoid(gb_ref[...].astype(F32)) * yb
    z = _dot(y.astype(BF16), wo_ref[...])
    o_ref[...] = x_ref[...] + mod_ref[0, 2:3, :] * z


def _merge_project(y_a, y_b, proj, x2d, mod3, mod_row, w_pa, w_pb, w_out, tm=512):
    t, d = x2d.shape
    wa = y_a.shape[1]
    wb = y_b.shape[1]
    const = dict(pipeline_mode=pl.Buffered(1))
    return pl.pallas_call(
        _merge_kernel,
        grid=(t // tm,),
        in_specs=[pl.BlockSpec((tm, wa), lambda i: (i, 0)),
                  pl.BlockSpec((tm, wb), lambda i: (i, 0)),
                  pl.BlockSpec((tm, d), lambda i: (i, 0)),
                  pl.BlockSpec((tm, d), lambda i: (i, 1)),
                  pl.BlockSpec((tm, d), lambda i: (i, 0)),
                  pl.BlockSpec((1,) + mod3.shape[1:], lambda i: (mod_row(i * tm), 0, 0)),
                  pl.BlockSpec((wa, d), lambda i: (0, 0), **const),
                  pl.BlockSpec((wb, d), lambda i: (0, 0), **const),
                  pl.BlockSpec((d, d), lambda i: (0, 0), **const)],
        out_specs=pl.BlockSpec((tm, d), lambda i: (i, 0)),
        out_shape=jax.ShapeDtypeStruct((t, d), F32),
        compiler_params=_params("arbitrary"),
        name="merge_project",
    )(y_a, y_b, proj, proj, x2d, mod3, w_pa, w_pb, w_out)


def _ffn_kernel(x_ref, xp_ref, xn_ref, mod_ref, n2_ref, wa_ref, wb_ref, cwa_ref, cwb_ref,
                cba_ref, cbb_ref, wd_ref, fn_ref, o_ref, h_scr, acc_scr, ua_scr, ub_scr,
                *, tm, seq):
    i = pl.program_id(0)
    j = pl.program_id(1)
    halo = BF16_ROWS

    @pl.when(j == 0)
    def _():
        def nrm(x):
            return _modulated_norm(x, n2_ref[...], mod_ref[0, 4:5, :], mod_ref[0, 3:4, :])
        starts_seq = (i * tm) % seq == 0
        ends_seq = ((i + 1) * tm) % seq == 0
        h_scr[0:halo, :] = jnp.where(starts_seq, 0.0, nrm(xp_ref[...])).astype(BF16)
        h_scr[halo:halo + tm, :] = nrm(x_ref[...]).astype(BF16)
        h_scr[halo + tm:, :] = jnp.where(ends_seq, 0.0, nrm(xn_ref[...])).astype(BF16)
        acc_scr[...] = jnp.zeros(acc_scr.shape, F32)

    h = h_scr[...]
    ua_scr[...] = _dot(h, wa_ref[...])
    ub_scr[...] = _dot(h, wb_ref[...])

    inner = range(seq, tm, seq)
    sub = lax.broadcasted_iota(jnp.int32, (8, 1), 0)

    def window(u_scr, shift, drop_rows, drop_sub):
        parts = []
        r0 = 0
        for r in drop_rows:
            slab = r // 8 * 8
            if slab > r0:
                parts.append(u_scr[halo + shift + r0:halo + shift + slab, :])
            parts.append(jnp.where(sub == drop_sub, 0.0,
                                   u_scr[halo + shift + slab:halo + shift + slab + 8, :]))
            r0 = slab + 8
        parts.append(u_scr[halo + shift + r0:halo + shift + tm, :])
        return parts[0] if len(parts) == 1 else jnp.concatenate(parts, axis=0)

    def conv(u_scr, cw_ref, cb_ref):
        prev = window(u_scr, -1, list(inner), 0)
        cur = u_scr[halo:halo + tm, :]
        nxt = window(u_scr, 1, [r - 1 for r in inner], 7)
        return prev * cw_ref[0:1, :] + cur * cw_ref[1:2, :] + nxt * cw_ref[2:3, :] + cb_ref[...]

    a = conv(ua_scr, cwa_ref, cba_ref)
    b = conv(ub_scr, cwb_ref, cbb_ref)
    g = (a * _sigmoid(a)) * b
    acc_scr[...] += _dot(g.astype(BF16), wd_ref[...])

    @pl.when(j == pl.num_programs(1) - 1)
    def _():
        x2 = x_ref[...] + mod_ref[0, 5:6, :] * acc_scr[...]
        ms = jnp.mean(x2 * x2, axis=-1, keepdims=True)
        o_ref[...] = x2 * lax.rsqrt(ms + EPS) * fn_ref[...]


def _conv_ffn(x2d, mod3, mod_row, norm2, w_up, conv_w, conv_b, w_down, final_norm, seq,
              tm=512, tf=512):
    t, d = x2d.shape
    dff = w_down.shape[0]
    nf = dff // tf
    halo = BF16_ROWS
    nhalo = t // halo
    per = tm // halo
    assert tm % seq == 0 or seq % tm == 0, "sequence boundaries must sit at static rows of a tile"
    return pl.pallas_call(
        functools.partial(_ffn_kernel, tm=tm, seq=seq),
        grid=(t // tm, nf),
        in_specs=[pl.BlockSpec((tm, d), lambda i, j: (i, 0)),
                  pl.BlockSpec((halo, d), lambda i, j: (jnp.maximum(i * per - 1, 0), 0)),
                  pl.BlockSpec((halo, d), lambda i, j: (jnp.minimum((i + 1) * per, nhalo - 1), 0)),
                  pl.BlockSpec((1,) + mod3.shape[1:], lambda i, j: (mod_row(i * tm), 0, 0)),
                  pl.BlockSpec((1, d), lambda i, j: (0, 0)),
                  pl.BlockSpec((d, tf), lambda i, j: (0, j)),
                  pl.BlockSpec((d, tf), lambda i, j: (0, nf + j)),
                  pl.BlockSpec((3, tf), lambda i, j: (0, j)),
                  pl.BlockSpec((3, tf), lambda i, j: (0, nf + j)),
                  pl.BlockSpec((1, tf), lambda i, j: (0, j)),
                  pl.BlockSpec((1, tf), lambda i, j: (0, nf + j)),
                  pl.BlockSpec((tf, d), lambda i, j: (j, 0)),
                  pl.BlockSpec((1, d), lambda i, j: (0, 0))],
        out_specs=pl.BlockSpec((tm, d), lambda i, j: (i, 0)),
        out_shape=jax.ShapeDtypeStruct((t, d), F32),
        scratch_shapes=[pltpu.VMEM((tm + 2 * halo, d), BF16), pltpu.VMEM((tm, d), F32),
                        pltpu.VMEM((tm + 2 * halo, tf), F32), pltpu.VMEM((tm + 2 * halo, tf), F32)],
        compiler_params=_params("arbitrary", "arbitrary"),
        name="conv_ffn",
    )(x2d, x2d, x2d, mod3, norm2, w_up, w_up, conv_w, conv_w, conv_b, conv_b, w_down, final_norm)


def _rope_tables(n_tokens):
    rows = n_tokens // GRID_W
    row = jnp.repeat(jnp.arange(rows), GRID_W).astype(F32)
    col = jnp.tile(jnp.arange(GRID_W), rows).astype(F32)
    quarter = DH_B // 4
    inv_freq = jnp.power(ROPE_BASE, -jnp.arange(quarter, dtype=F32) / quarter)
    ang_r = row[:, None] * inv_freq
    ang_c = col[:, None] * inv_freq
    ang = jnp.concatenate([ang_r, ang_r, ang_c, ang_c] * 2, axis=-1)
    sign = jnp.where((jnp.arange(2 * DH_B) // quarter) % 2 == 0, -1.0, 1.0).astype(F32)
    return jnp.cos(ang), jnp.sin(ang) * sign


def _gate_columns(g):
    lead = g.shape[:-1]
    g4 = g.reshape(lead + (4, H_A))
    pad = jnp.zeros(lead + (LANES - 2 * H_A,), g.dtype)
    return jnp.concatenate([g4[..., 0, :], g4[..., 2, :], pad, g4[..., 1, :], g4[..., 3, :], pad],
                           axis=-1)


def _trunk(x2d, batch, seq, mod3, mod_row, wts, lambda_init, chunk, cache=None, state=None,
           rope_tables=None, emit_state=False):
    t, d = x2d.shape
    span = t if cache is None else seq
    proj, qkv, gates = _in_projection(x2d, mod3, mod_row, wts["norm1"], wts["w_main"],
                                      wts["w_gates"], cols16=wts["cols16"], tm=min(1024, span))
    a_out = _mlstm(proj, 2 * d, gates, wts["b_gates"], wts["mlstm_norm"], batch, seq, chunk,
                   state=state, emit_state=emit_state)
    if cache is None:
        b_out = _attention_context(qkv, wts["lam"], wts["diff_norm"], batch, seq, lambda_init)
    else:
        b_out = [_attention_latent(qkv, wts["lam"], wts["diff_norm"], batch, seq, min(2048, seq), lambda_init,
                                   cache, rope_tables)]
    x1 = _merge_project(a_out[0], b_out[0], proj, x2d, mod3, mod_row, wts["w_pa"], wts["w_pb"],
                        wts["w_out"], tm=min(512, span))
    y = _conv_ffn(x1, mod3, mod_row, wts["norm2"], wts["w_up"], wts["conv_w"], wts["conv_b"],
                  wts["w_down"], wts["final_norm"], seq, tm=min(512, span))
    return y, a_out[1:], b_out[1:]


def kernel(x_prompt, x_sample, cache_k, cache_v, state_C, state_n, state_m, c, c_ctx, w_mod, b_mod, norm1, w_in, b_gates, mlstm_norm, lam_q1, lam_k1, lam_q2, lam_k2, diff_norm, w_pa, w_pb, w_out, norm2, w_up, conv_w, conv_b, w_down, final_norm):
    assert w_in.shape[0] == 1, "single trunk layer"
    bp, sp, d = x_prompt.shape
    bs, ss, _ = x_sample.shape
    past = cache_k.shape[3]
    lambda_init = 0.8 - 0.6 * math.exp(-0.3 * 0)

    wa_cols = 2 * H_A * DQK_A + 2 * H_A * DV_A
    wb_cols = 3 * H_B * 2 * DH_B
    g0 = wa_cols
    b0 = g0 + 4 * H_A
    m0 = b0 + wb_cols
    w = w_in.reshape(w_in.shape[1:])
    wts = {
        "w_main": jnp.concatenate([w[:, m0:].astype(BF16), w[:, :g0].astype(BF16),
                                   w[:, b0:m0].astype(BF16)], axis=1),
        "cols16": 2 * d + wa_cols,
        "w_gates": _gate_columns(w[:, g0:b0]).astype(BF16),
        "b_gates": _gate_columns(b_gates.reshape(-1))[None, :],
        "norm1": norm1, "norm2": norm2, "mlstm_norm": mlstm_norm, "diff_norm": diff_norm,
        "lam": jnp.concatenate([lam_q1, lam_k1, lam_q2, lam_k2], axis=0),
        "w_pa": w_pa.reshape(w_pa.shape[1:]).astype(BF16),
        "w_pb": w_pb.reshape(w_pb.shape[1:]).astype(BF16),
        "w_out": w_out.reshape(w_out.shape[1:]).astype(BF16),
        "w_up": w_up.reshape(w_up.shape[1:]).astype(BF16),
        "conv_w": conv_w.reshape(conv_w.shape[1:]), "conv_b": conv_b,
        "w_down": w_down.reshape(w_down.shape[1:]).astype(BF16),
        "final_norm": final_norm[None, :],
    }

    c_all = jnp.concatenate([c_ctx[None, :], c, jnp.zeros((8 - 1 - bs, d), F32)], axis=0)
    mod3 = _modulation(c_all, w_mod.reshape(w_mod.shape[1:]), b_mod).reshape(8, 6, d)

    yp, (c_new, n_new, m_new), (k_new, v_new) = _trunk(
        x_prompt.reshape(bp * sp, d), bp, sp, mod3, lambda t: 0, wts, lambda_init,
        chunk=sp, emit_state=True)

    state = (state_C.reshape(bs, 2, H_A, DQK_A, DV_A), state_n.reshape(bs, 2, H_A // 2, LANES),
             state_m.reshape(bs, 2, H_A))
    cache = (cache_k.reshape(bs, H_B, past, 2 * DH_B), cache_v.reshape(bs, H_B, past, DV_B))
    ys, _, _ = _trunk(
        x_sample.reshape(bs * ss, d), bs, ss, mod3, lambda t: 1 + t // ss, wts, lambda_init,
        chunk=256, cache=cache, state=state, rope_tables=_rope_tables(ss))

    return (yp.reshape(bp, sp, d), ys.reshape(bs, ss, d),
            k_new.reshape(bp, 1, H_B, sp, 2 * DH_B), v_new.reshape(bp, 1, H_B, sp, DV_B),
            c_new.reshape(bp, 1, 2, H_A, DQK_A, DV_A), n_new.reshape(bp, 1, 2, H_A, DQK_A),
            m_new.reshape(bp, 1, 2, H_A))
```

```python
import functools
import math

import jax
import jax.numpy as jnp
from jax import lax
from jax.experimental import pallas as pl
from jax.experimental.pallas import tpu as pltpu

F32 = jnp.float32
BF16 = jnp.bfloat16

GRID_W = 64
H_A = 8
DQK_A = 64
DV_A = 128
H_B = 8
DH_B = 64
DV_B = 128
ROPE_BASE = 10000.0
GATE_CAP = 15.0
EPS = 1e-6
NEG_BIG = -1e30
LOG2_E = 1.4426950408889634

LANES = 128
BF16_ROWS = 16
MXU_COLS = 256
VMEM_LIMIT = 56 * 1024 * 1024


def _sigmoid(x):
    return 1.0 / (1.0 + jnp.exp(-x))


def _dot(a, b):
    return jnp.dot(a, b, preferred_element_type=F32)


def _dot_nt(a, b):
    return lax.dot_general(a, b, (((1,), (1,)), ((), ())), preferred_element_type=F32)


def _dot_tn(a, b):
    return lax.dot_general(a, b, (((0,), (0,)), ((), ())), preferred_element_type=F32)


def _tile(n, want):
    t = min(want, n)
    while n % t:
        t -= LANES
    return t


def _params(*sem):
    return pltpu.CompilerParams(dimension_semantics=sem, vmem_limit_bytes=VMEM_LIMIT)


def _mod_kernel(c_ref, w_ref, b_ref, o_ref):
    c = c_ref[...]
    s = (c * _sigmoid(c)).astype(BF16)
    o_ref[...] = _dot(s, w_ref[...].astype(BF16)) + b_ref[...]


def _modulation(c_all, w_mod, b_mod, tn=1024):
    rows, d = c_all.shape
    n = w_mod.shape[1]
    tn = _tile(n, tn)
    return pl.pallas_call(
        _mod_kernel,
        grid=(n // tn,),
        in_specs=[pl.BlockSpec((rows, d), lambda j: (0, 0)),
                  pl.BlockSpec((d, tn), lambda j: (0, j)),
                  pl.BlockSpec((1, tn), lambda j: (0, j))],
        out_specs=pl.BlockSpec((rows, tn), lambda j: (0, j)),
        out_shape=jax.ShapeDtypeStruct((rows, n), F32),
        compiler_params=_params("arbitrary"),
        name="modulation",
    )(c_all, w_mod, b_mod)


def _modulated_norm(x, w, scale, shift):
    ms = jnp.mean(x * x, axis=-1, keepdims=True)
    return (x * lax.rsqrt(ms + EPS) * w) * (1.0 + scale) + shift


def _inproj_kernel(x_ref, mod_ref, n1_ref, w_ref, wg_ref, o16_ref, o32_ref, g_ref, h_scr, *, n16):
    j = pl.program_id(1)

    @pl.when(j == 0)
    def _():
        h = _modulated_norm(x_ref[...], n1_ref[...], mod_ref[0, 1:2, :], mod_ref[0, 0:1, :])
        hb = h.astype(BF16)
        h_scr[...] = hb
        g_ref[...] = _dot(hb, wg_ref[...])

    @pl.when(j < n16)
    def _():
        o16_ref[...] = _dot(h_scr[...], w_ref[...]).astype(BF16)

    @pl.when(j >= n16)
    def _():
        o32_ref[...] = _dot(h_scr[...], w_ref[...])


def _in_projection(x2d, mod3, mod_row, norm1, w_main, w_gates, cols16, tm=1024, tn=1024):
    t, d = x2d.shape
    n = w_main.shape[1]
    ng = w_gates.shape[1]
    tn = _tile(math.gcd(cols16, n - cols16), tn)
    n16 = cols16 // tn
    n32 = (n - cols16) // tn
    return pl.pallas_call(
        functools.partial(_inproj_kernel, n16=n16),
        grid=(t // tm, n16 + n32),
        in_specs=[pl.BlockSpec((tm, d), lambda i, j: (i, 0)),
                  pl.BlockSpec((1,) + mod3.shape[1:], lambda i, j: (mod_row(i * tm), 0, 0)),
                  pl.BlockSpec((1, d), lambda i, j: (0, 0)),
                  pl.BlockSpec((d, tn), lambda i, j: (0, j)),
                  pl.BlockSpec((d, ng), lambda i, j: (0, 0))],
        out_specs=[pl.BlockSpec((tm, tn), lambda i, j: (i, jnp.minimum(j, n16 - 1))),
                   pl.BlockSpec((tm, tn), lambda i, j: (i, jnp.maximum(j - n16, 0))),
                   pl.BlockSpec((tm, ng), lambda i, j: (i, 0))],
        out_shape=[jax.ShapeDtypeStruct((t, cols16), BF16),
                   jax.ShapeDtypeStruct((t, n - cols16), F32),
                   jax.ShapeDtypeStruct((t, ng), F32)],
        scratch_shapes=[pltpu.VMEM((tm, d), BF16)],
        compiler_params=_params("arbitrary", "arbitrary"),
        name="in_projection",
    )(x2d, mod3, norm1, w_main, w_gates)


def _split3(x):
    hi = x.astype(BF16)
    r = x - hi.astype(F32)
    mid = r.astype(BF16)
    lo = (r - mid.astype(F32)).astype(BF16)
    return hi, mid, lo


def _mlstm_kernel(*refs, seq, chunk, has_state, emit_state):
    q_ref, k_ref, v_ref, o_ref, g_ref, bg_ref, nrm_ref = refs[:7]
    pos = 7
    if has_state:
        c0_ref, n0_ref, m0_ref = refs[pos:pos + 3]
        pos += 3
    y_ref = refs[pos]
    pos += 1
    if emit_state:
        co_ref, no_ref, mo_ref = refs[pos:pos + 3]
        pos += 3
    bc_scr, a_scr, at_scr, hacc_scr = refs[pos:pos + 4]

    L = chunk
    nc = seq // L
    p = pl.program_id(1)

    @pl.when(p == 0)
    def _():
        row = lax.broadcasted_iota(jnp.int32, (L, L), 0)
        col = lax.broadcasted_iota(jnp.int32, (L, L), 1)
        tril = jnp.where(col <= row, 1.0, 0.0).astype(BF16)
        triu = jnp.where(col >= row, 1.0, 0.0).astype(BF16)
        lane = lax.broadcasted_iota(jnp.int32, (L, LANES), 1)
        for c in range(nc):
            rows = pl.ds(c * L, L)
            g = g_ref[rows, :] + bg_ref[...]
            g = GATE_CAP * jnp.tanh(g / GATE_CAP)
            gi = g[:, :LANES]
            gf = g[:, LANES:]
            lf = jnp.minimum(gf, 0.0) - jnp.log(1.0 + jnp.exp(-jnp.abs(gf)))
            hi, mid, lo = _split3(lf)
            pre = _dot(tril, hi) + _dot(tril, mid) + _dot(tril, lo)
            suf = _dot(triu, hi) + _dot(triu, mid) + _dot(triu, lo)
            bc = jnp.where(lane < H_A, pre, suf)
            a = gi - bc
            bc_scr[rows, :] = bc
            a_scr[rows, :] = a
            at_scr[c] = a.T
        if emit_state:
            mo_ref[...] = jnp.zeros(mo_ref.shape, F32)

    lane_row = lax.broadcasted_iota(jnp.int32, (1, LANES), 1)
    lane_l = lax.broadcasted_iota(jnp.int32, (L, LANES), 1)
    trow = lax.broadcasted_iota(jnp.int32, (L, L), 0)
    tcol = lax.broadcasted_iota(jnp.int32, (L, L), 1)
    m_rows = lax.broadcasted_iota(jnp.int32, (2, H_A), 0)
    m_cols = lax.broadcasted_iota(jnp.int32, (2, H_A), 1)

    chains = [(e, d) for e in (0, 1) for d in (0, 1)]

    def head_mask(e):
        return (lane_row >= DQK_A) if e else (lane_row < DQK_A)

    def initial_state(e, d):
        if not has_state:
            return (jnp.zeros((2 * DQK_A, DV_A), F32), jnp.zeros((1, LANES), F32),
                    jnp.zeros((1, 1), F32))
        c0 = c0_ref[0, d, e]
        z = jnp.zeros_like(c0)
        sel_m = (m_rows == d) & (m_cols == 2 * p + e)
        m0 = jnp.sum(jnp.sum(jnp.where(sel_m, m0_ref[0], 0.0), axis=1, keepdims=True),
                     axis=0, keepdims=True)
        return (jnp.concatenate([z, c0] if e else [c0, z], axis=0),
                jnp.where(head_mask(e), n0_ref[0, d, pl.ds(p, 1), :], 0.0), m0)

    def chunk_rows(c):
        r0 = c * L
        return pl.ds(r0 if isinstance(r0, int) else pl.multiple_of(r0, L), L)

    def chunk_step(ci, carry, e, d):
        cst, nrow, m = carry
        c = ci if d == 0 else nc - 1 - ci
        rows = chunk_rows(c)
        hmask = head_mask(e)
        vcols = slice(e * DV_A, (e + 1) * DV_A)
        colidx = 2 * p + e + H_A * d
        causal = (tcol <= trow) if d == 0 else (tcol >= trow)
        qm = jnp.where(hmask, q_ref[rows, :].astype(F32) * (DQK_A ** -0.5), 0.0)
        qmb = qm.astype(BF16)
        kb = k_ref[rows, :]
        vh = v_ref[rows, vcols]
        s = _dot_nt(qmb, kb)
        pick = lane_l == colidx
        bcol = jnp.sum(jnp.where(pick, bc_scr[rows, :], 0.0), axis=1, keepdims=True)
        acol = jnp.sum(jnp.where(pick, a_scr[rows, :], 0.0), axis=1, keepdims=True)
        arow = at_scr[c, pl.ds(colidx, 1), :]
        logd = jnp.where(causal, bcol + arow, NEG_BIG)
        m_inter = bcol + m
        m_t = jnp.maximum(m_inter, jnp.max(logd, axis=1, keepdims=True))
        pmat = s * jnp.exp(logd - m_t)
        inter = jnp.exp(m_inter - m_t)
        num = _dot(pmat.astype(BF16), vh) + inter * _dot(qmb, cst.astype(BF16))
        den = (jnp.sum(pmat, axis=1, keepdims=True)
               + inter * jnp.sum(qm * nrow, axis=1, keepdims=True))
        hacc_scr[d, e, rows, :] = num / jnp.maximum(jnp.abs(den), jnp.exp(-m_t))
        b_last = bcol[L - 1:L, :] if d == 0 else bcol[0:1, :]
        logw = b_last + acol
        m_new = jnp.maximum(b_last + m, jnp.max(logw, axis=0, keepdims=True))
        wk = jnp.exp(logw - m_new) * jnp.where(hmask, kb.astype(F32), 0.0)
        decay = jnp.exp(b_last + m - m_new)
        cst = decay * cst + _dot_tn(wk.astype(BF16), vh)
        nrow = decay * nrow + jnp.sum(wk, axis=0, keepdims=True)
        return cst, nrow, m_new

    def all_chains(ci, carry):
        return tuple(chunk_step(ci, st, e, d) for (e, d), st in zip(chains, carry))

    def finish(c, _):
        rows = chunk_rows(c)
        for e in (0, 1):
            vcols = slice(e * DV_A, (e + 1) * DV_A)
            ht = hacc_scr[0, e, rows, :] + hacc_scr[1, e, rows, :]
            ms = jnp.mean(ht * ht, axis=1, keepdims=True)
            hn = ht * lax.rsqrt(ms + EPS) * nrm_ref[:, vcols]
            og = o_ref[rows, vcols].astype(F32)
            y_ref[rows, vcols] = (hn * _sigmoid(og)).astype(BF16)
        return 0

    init = tuple(initial_state(e, d) for e, d in chains)
    if nc == 1:
        final = all_chains(0, init)
        finish(0, 0)
    else:
        final = lax.fori_loop(0, nc, all_chains, init)
        lax.fori_loop(0, nc, finish, 0)

    if emit_state:
        state = dict(zip(chains, final))
        for (e, d), (cst, _, m) in state.items():
            co_ref[0, d, e] = cst[e * DQK_A:(e + 1) * DQK_A, :]
            sel_m = (m_rows == d) & (m_cols == 2 * p + e)
            mo_ref[0] = jnp.where(sel_m, m, mo_ref[0])
        for d in (0, 1):
            no_ref[0, d, pl.ds(p, 1), :] = state[(0, d)][1] + state[(1, d)][1]


def _mlstm(proj, col0, gates, b_gates2, mlstm_norm, batch, seq, chunk, state=None,
           emit_state=False):
    npairs = H_A // 2
    qblk = col0 // LANES
    kblk = qblk + (H_A * DQK_A) // LANES
    vblk = (col0 + 2 * H_A * DQK_A) // (2 * DV_A)
    oblk = vblk + (H_A * DV_A) // (2 * DV_A)
    ng = gates.shape[1]
    in_specs = [pl.BlockSpec((seq, LANES), lambda b, p: (b, qblk + p)),
                pl.BlockSpec((seq, LANES), lambda b, p: (b, kblk + p)),
                pl.BlockSpec((seq, 2 * DV_A), lambda b, p: (b, vblk + p)),
                pl.BlockSpec((seq, 2 * DV_A), lambda b, p: (b, oblk + p)),
                pl.BlockSpec((seq, ng), lambda b, p: (b, 0)),
                pl.BlockSpec((1, ng), lambda b, p: (0, 0)),
                pl.BlockSpec((1, 2 * DV_A), lambda b, p: (0, p))]
    args = [proj, proj, proj, proj, gates, b_gates2, mlstm_norm]
    if state is not None:
        c0, n0, m0 = state
        in_specs += [pl.BlockSpec((1, 2, 2, DQK_A, DV_A), lambda b, p: (b, 0, p, 0, 0)),
                     pl.BlockSpec((1, 2, npairs, LANES), lambda b, p: (b, 0, 0, 0)),
                     pl.BlockSpec((1, 2, H_A), lambda b, p: (b, 0, 0))]
        args += [c0, n0, m0]
    out_specs = [pl.BlockSpec((seq, 2 * DV_A), lambda b, p: (b, p))]
    out_shape = [jax.ShapeDtypeStruct((batch * seq, H_A * DV_A), BF16)]
    if emit_state:
        out_specs += [pl.BlockSpec((1, 2, 2, DQK_A, DV_A), lambda b, p: (b, 0, p, 0, 0)),
                      pl.BlockSpec((1, 2, npairs, LANES), lambda b, p: (b, 0, 0, 0)),
                      pl.BlockSpec((1, 2, H_A), lambda b, p: (b, 0, 0))]
        out_shape += [jax.ShapeDtypeStruct((batch, 2, H_A, DQK_A, DV_A), F32),
                      jax.ShapeDtypeStruct((batch, 2, npairs, LANES), F32),
                      jax.ShapeDtypeStruct((batch, 2, H_A), F32)]
    return pl.pallas_call(
        functools.partial(_mlstm_kernel, seq=seq, chunk=chunk, has_state=state is not None,
                          emit_state=emit_state),
        grid=(batch, npairs),
        in_specs=in_specs,
        out_specs=out_specs,
        out_shape=out_shape,
        scratch_shapes=[pltpu.VMEM((seq, LANES), F32), pltpu.VMEM((seq, LANES), F32),
                        pltpu.VMEM((seq // chunk, LANES, chunk), F32),
                        pltpu.VMEM((2, 2, seq, DV_A), F32)],
        compiler_params=_params("arbitrary", "arbitrary"),
        name="mlstm",
    )(*args)


def _rope(x, cos, sin_signed, lane):
    ahead = pltpu.roll(x, LANES - DH_B // 4, axis=1)
    behind = pltpu.roll(x, DH_B // 4, axis=1)
    even_quarter = (lane // (DH_B // 4)) % 2 == 0
    return x * cos + jnp.where(even_quarter, ahead, behind) * sin_signed


def _ones_column(rows):
    lane = lax.broadcasted_iota(jnp.int32, (rows, LANES), 1)
    return jnp.where(lane == 0, 1.0, 0.0).astype(BF16)


def _lambda(lam_ref, lambda_init):
    lp = lam_ref[...]
    return (jnp.exp(jnp.sum(lp[0:1] * lp[1:2], axis=1, keepdims=True))
            - jnp.exp(jnp.sum(lp[2:3] * lp[3:4], axis=1, keepdims=True)) + lambda_init)


def _diff_attend(q, keys, v_aug, lam, dn, lambda_init, single_pv=False):
    lane = lax.broadcasted_iota(jnp.int32, (1, LANES), 1)
    q = q * (DH_B ** -0.5 * LOG2_E)

    def softmax_pv(s):
        ex = jnp.exp2(s - jnp.max(s, axis=1, keepdims=True)).astype(BF16)
        r = _dot(ex, v_aug)
        return r[:, :DV_B] * (1.0 / r[:, DV_B:DV_B + 1])

    def expo(s):
        ex = jnp.exp2(s - jnp.max(s, axis=1, keepdims=True))
        return ex, 1.0 / jnp.sum(ex, axis=1, keepdims=True)

    s1 = _dot_nt(jnp.where(lane < DH_B, q, 0.0).astype(BF16), keys)
    s2 = _dot_nt(jnp.where(lane >= DH_B, q, 0.0).astype(BF16), keys)
    if single_pv:
        e1, r1 = expo(s1)
        e2, r2 = expo(s2)
        o = _dot((e1 * r1 - e2 * (lam * r2)).astype(BF16), v_aug[:, :DV_B])
    else:
        o = softmax_pv(s1) - lam * softmax_pv(s2)
    ms = jnp.mean(o * o, axis=1, keepdims=True)
    return (o * lax.rsqrt(ms + EPS) * dn) * (1.0 - lambda_init)


def _attn_latent_kernel(q_ref, k_ref, v_ref, ck_ref, cv_ref, cos_ref, sin_ref, lam_ref, dn_ref,
                        o_ref, k_scr, v_scr, *, tq, sub, seq, lambda_init):
    qi = pl.program_id(2)
    lane = lax.broadcasted_iota(jnp.int32, (1, LANES), 1)

    @pl.when(qi == 0)
    def _():
        k_scr[0:seq, :] = _rope(k_ref[...], cos_ref[...], sin_ref[...], lane).astype(BF16)
        k_scr[seq:, :] = ck_ref[0, 0].astype(BF16)
        v_scr[0:seq, 0:DV_B] = v_ref[...].astype(BF16)
        v_scr[seq:, 0:DV_B] = cv_ref[0, 0].astype(BF16)
        v_scr[:, DV_B:] = _ones_column(v_scr.shape[0])

    lam = _lambda(lam_ref, lambda_init)
    for r0 in range(0, tq, sub):
        rows = pl.ds(pl.multiple_of(qi * tq + r0, sub), sub)
        q = _rope(q_ref[r0:r0 + sub, :], cos_ref[rows, :], sin_ref[rows, :], lane)
        o = _diff_attend(q, k_scr[...], v_scr[...], lam, dn_ref[...], lambda_init)
        o_ref[r0:r0 + sub, :] = o.astype(BF16)


def _attn_context_kernel(q_ref, k_ref, v_ref, lam_ref, dn_ref, o_ref, nk_ref, nv_ref, *,
                         lambda_init):
    lam = _lambda(lam_ref, lambda_init)
    ones = _ones_column(q_ref.shape[0])
    for h in range(H_B):
        cols = slice(h * LANES, (h + 1) * LANES)
        k = k_ref[:, cols]
        v = v_ref[:, cols]
        nk_ref[0, h] = k
        nv_ref[0, h] = v
        v_aug = jnp.concatenate([v.astype(BF16), ones], axis=1)
        o = _diff_attend(q_ref[:, cols], k.astype(BF16), v_aug, lam, dn_ref[...], lambda_init,
                         single_pv=True)
        o_ref[:, cols] = o.astype(BF16)


def _attention_context(qkv, lam_params, diff_norm, batch, seq, lambda_init):
    w = H_B * LANES
    return pl.pallas_call(
        functools.partial(_attn_context_kernel, lambda_init=lambda_init),
        grid=(batch,),
        in_specs=[pl.BlockSpec((seq, w), lambda b: (b, 0)),
                  pl.BlockSpec((seq, w), lambda b: (b, 1)),
                  pl.BlockSpec((seq, w), lambda b: (b, 2)),
                  pl.BlockSpec(lam_params.shape, lambda b: (0, 0)),
                  pl.BlockSpec((1, DV_B), lambda b: (0, 0))],
        out_specs=[pl.BlockSpec((seq, w), lambda b: (b, 0)),
                   pl.BlockSpec((1, H_B, seq, LANES), lambda b: (b, 0, 0, 0)),
                   pl.BlockSpec((1, H_B, seq, DV_B), lambda b: (b, 0, 0, 0))],
        out_shape=[jax.ShapeDtypeStruct((batch * seq, w), BF16),
                   jax.ShapeDtypeStruct((batch, H_B, seq, LANES), F32),
                   jax.ShapeDtypeStruct((batch, H_B, seq, DV_B), F32)],
        compiler_params=_params("arbitrary"),
        name="attention_context",
    )(qkv, qkv, qkv, lam_params, diff_norm)


def _attention_latent(qkv, lam_params, diff_norm, batch, seq, tq, lambda_init, cache, rope_tables):
    ck, cv = cache
    past = ck.shape[2]
    cos, sin_signed = rope_tables
    nq = seq // tq
    return pl.pallas_call(
        functools.partial(_attn_latent_kernel, tq=tq, sub=min(256, tq), seq=seq,
                          lambda_init=lambda_init),
        grid=(batch, H_B, nq),
        in_specs=[pl.BlockSpec((tq, LANES), lambda b, h, i: (b * nq + i, h)),
                  pl.BlockSpec((seq, LANES), lambda b, h, i: (b, H_B + h)),
                  pl.BlockSpec((seq, LANES), lambda b, h, i: (b, 2 * H_B + h)),
                  pl.BlockSpec((1, 1, past, LANES), lambda b, h, i: (b, h, 0, 0)),
                  pl.BlockSpec((1, 1, past, LANES), lambda b, h, i: (b, h, 0, 0)),
                  pl.BlockSpec((seq, LANES), lambda b, h, i: (0, 0)),
                  pl.BlockSpec((seq, LANES), lambda b, h, i: (0, 0)),
                  pl.BlockSpec(lam_params.shape, lambda b, h, i: (0, 0)),
                  pl.BlockSpec((1, DV_B), lambda b, h, i: (0, 0))],
        out_specs=pl.BlockSpec((tq, DV_B), lambda b, h, i: (b * nq + i, h)),
        out_shape=jax.ShapeDtypeStruct((batch * seq, H_B * DV_B), BF16),
        scratch_shapes=[pltpu.VMEM((seq + past, LANES), BF16),
                        pltpu.VMEM((seq + past, 2 * DV_B), BF16)],
        compiler_params=_params("arbitrary", "arbitrary", "arbitrary"),
        name="attention_latent",
    )(qkv, qkv, qkv, ck, cv, cos, sin_signed, lam_params, diff_norm)


def _merge_kernel(ya_ref, yb_ref, ga_ref, gb_ref, x_ref, mod_ref, wpa_ref, wpb_ref, wo_ref, o_ref):
    ya = _dot(ya_ref[...], wpa_ref[...])
    yb = _dot(yb_ref[...], wpb_ref[...])
    y = _sigmoid(ga_ref[...].astype(F32)) * ya + _sigmoid(gb_ref[...].astype(F32)) * yb
    z = _dot(y.astype(BF16), wo_ref[...])
    o_ref[...] = x_ref[...] + mod_ref[0, 2:3, :] * z


def _merge_project(y_a, y_b, proj, x2d, mod3, mod_row, w_pa, w_pb, w_out, tm=512):
    t, d = x2d.shape
    wa = y_a.shape[1]
    wb = y_b.shape[1]
    const = dict(pipeline_mode=pl.Buffered(1))
    return pl.pallas_call(
        _merge_kernel,
        grid=(t // tm,),
        in_specs=[pl.BlockSpec((tm, wa), lambda i: (i, 0)),
                  pl.BlockSpec((tm, wb), lambda i: (i, 0)),
                  pl.BlockSpec((tm, d), lambda i: (i, 0)),
                  pl.BlockSpec((tm, d), lambda i: (i, 1)),
                  pl.BlockSpec((tm, d), lambda i: (i, 0)),
                  pl.BlockSpec((1,) + mod3.shape[1:], lambda i: (mod_row(i * tm), 0, 0)),
                  pl.BlockSpec((wa, d), lambda i: (0, 0), **const),
                  pl.BlockSpec((wb, d), lambda i: (0, 0), **const),
                  pl.BlockSpec((d, d), lambda i: (0, 0), **const)],
        out_specs=pl.BlockSpec((tm, d), lambda i: (i, 0)),
        out_shape=jax.ShapeDtypeStruct((t, d), F32),
        compiler_params=_params("arbitrary"),
        name="merge_project",
    )(y_a, y_b, proj, proj, x2d, mod3, w_pa, w_pb, w_out)


def _ffn_kernel(x_ref, xp_ref, xn_ref, mod_ref, n2_ref, wa_ref, wb_ref, cwa_ref, cwb_ref,
                cba_ref, cbb_ref, wd_ref, fn_ref, o_ref, h_scr, acc_scr, ua_scr, ub_scr,
                *, tm, seq):
    i = pl.program_id(0)
    j = pl.program_id(1)
    halo = BF16_ROWS

    @pl.when(j == 0)
    def _():
        def nrm(x):
            return _modulated_norm(x, n2_ref[...], mod_ref[0, 4:5, :], mod_ref[0, 3:4, :])
        starts_seq = (i * tm) % seq == 0
        ends_seq = ((i + 1) * tm) % seq == 0
        h_scr[0:halo, :] = jnp.where(starts_seq, 0.0, nrm(xp_ref[...])).astype(BF16)
        h_scr[halo:halo + tm, :] = nrm(x_ref[...]).astype(BF16)
        h_scr[halo + tm:, :] = jnp.where(ends_seq, 0.0, nrm(xn_ref[...])).astype(BF16)
        acc_scr[...] = jnp.zeros(acc_scr.shape, F32)

    h = h_scr[...]
    ua_scr[...] = _dot(h, wa_ref[...])
    ub_scr[...] = _dot(h, wb_ref[...])

    inner = range(seq, tm, seq)
    sub = lax.broadcasted_iota(jnp.int32, (8, 1), 0)

    def window(u_scr, shift, drop_rows, drop_sub):
        parts = []
        r0 = 0
        for r in drop_rows:
            slab = r // 8 * 8
            if slab > r0:
                parts.append(u_scr[halo + shift + r0:halo + shift + slab, :])
            parts.append(jnp.where(sub == drop_sub, 0.0,
                                   u_scr[halo + shift + slab:halo + shift + slab + 8, :]))
            r0 = slab + 8
        parts.append(u_scr[halo + shift + r0:halo + shift + tm, :])
        return parts[0] if len(parts) == 1 else jnp.concatenate(parts, axis=0)

    def conv(u_scr, cw_ref, cb_ref):
        prev = window(u_scr, -1, list(inner), 0)
        cur = u_scr[halo:halo + tm, :]
        nxt = window(u_scr, 1, [r - 1 for r in inner], 7)
        return prev * cw_ref[0:1, :] + cur * cw_ref[1:2, :] + nxt * cw_ref[2:3, :] + cb_ref[...]

    a = conv(ua_scr, cwa_ref, cba_ref)
    b = conv(ub_scr, cwb_ref, cbb_ref)
    g = (a * _sigmoid(a)) * b
    acc_scr[...] += _dot(g.astype(BF16), wd_ref[...])

    @pl.when(j == pl.num_programs(1) - 1)
    def _():
        x2 = x_ref[...] + mod_ref[0, 5:6, :] * acc_scr[...]
        ms = jnp.mean(x2 * x2, axis=-1, keepdims=True)
        o_ref[...] = x2 * lax.rsqrt(ms + EPS) * fn_ref[...]


def _conv_ffn(x2d, mod3, mod_row, norm2, w_up, conv_w, conv_b, w_down, final_norm, seq,
              tm=512, tf=512):
    t, d = x2d.shape
    dff = w_down.shape[0]
    nf = dff // tf
    halo = BF16_ROWS
    nhalo = t // halo
    per = tm // halo
    assert tm % seq == 0 or seq % tm == 0, "sequence boundaries must sit at static rows of a tile"
    return pl.pallas_call(
        functools.partial(_ffn_kernel, tm=tm, seq=seq),
        grid=(t // tm, nf),
        in_specs=[pl.BlockSpec((tm, d), lambda i, j: (i, 0)),
                  pl.BlockSpec((halo, d), lambda i, j: (jnp.maximum(i * per - 1, 0), 0)),
                  pl.BlockSpec((halo, d), lambda i, j: (jnp.minimum((i + 1) * per, nhalo - 1), 0)),
                  pl.BlockSpec((1,) + mod3.shape[1:], lambda i, j: (mod_row(i * tm), 0, 0)),
                  pl.BlockSpec((1, d), lambda i, j: (0, 0)),
                  pl.BlockSpec((d, tf), lambda i, j: (0, j)),
                  pl.BlockSpec((d, tf), lambda i, j: (0, nf + j)),
                  pl.BlockSpec((3, tf), lambda i, j: (0, j)),
                  pl.BlockSpec((3, tf), lambda i, j: (0, nf + j)),
                  pl.BlockSpec((1, tf), lambda i, j: (0, j)),
                  pl.BlockSpec((1, tf), lambda i, j: (0, nf + j)),
                  pl.BlockSpec((tf, d), lambda i, j: (j, 0)),
                  pl.BlockSpec((1, d), lambda i, j: (0, 0))],
        out_specs=pl.BlockSpec((tm, d), lambda i, j: (i, 0)),
        out_shape=jax.ShapeDtypeStruct((t, d), F32),
        scratch_shapes=[pltpu.VMEM((tm + 2 * halo, d), BF16), pltpu.VMEM((tm, d), F32),
                        pltpu.VMEM((tm + 2 * halo, tf), F32), pltpu.VMEM((tm + 2 * halo, tf), F32)],
        compiler_params=_params("arbitrary", "arbitrary"),
        name="conv_ffn",
    )(x2d, x2d, x2d, mod3, norm2, w_up, w_up, conv_w, conv_w, conv_b, conv_b, w_down, final_norm)


def _rope_tables(n_tokens):
    rows = n_tokens // GRID_W
    row = jnp.repeat(jnp.arange(rows), GRID_W).astype(F32)
    col = jnp.tile(jnp.arange(GRID_W), rows).astype(F32)
    quarter = DH_B // 4
    inv_freq = jnp.power(ROPE_BASE, -jnp.arange(quarter, dtype=F32) / quarter)
    ang_r = row[:, None] * inv_freq
    ang_c = col[:, None] * inv_freq
    ang = jnp.concatenate([ang_r, ang_r, ang_c, ang_c] * 2, axis=-1)
    sign = jnp.where((jnp.arange(2 * DH_B) // quarter) % 2 == 0, -1.0, 1.0).astype(F32)
    return jnp.cos(ang), jnp.sin(ang) * sign


def _gate_columns(g):
    lead = g.shape[:-1]
    g4 = g.reshape(lead + (4, H_A))
    pad = jnp.zeros(lead + (LANES - 2 * H_A,), g.dtype)
    return jnp.concatenate([g4[..., 0, :], g4[..., 2, :], pad, g4[..., 1, :], g4[..., 3, :], pad],
                           axis=-1)


def _trunk(x2d, batch, seq, mod3, mod_row, wts, lambda_init, chunk, cache=None, state=None,
           rope_tables=None, emit_state=False):
    t, d = x2d.shape
    span = t if cache is None else seq
    proj, qkv, gates = _in_projection(x2d, mod3, mod_row, wts["norm1"], wts["w_main"],
                                      wts["w_gates"], cols16=wts["cols16"], tm=min(1024, span))
    a_out = _mlstm(proj, 2 * d, gates, wts["b_gates"], wts["mlstm_norm"], batch, seq, chunk,
                   state=state, emit_state=emit_state)
    if cache is None:
        b_out = _attention_context(qkv, wts["lam"], wts["diff_norm"], batch, seq, lambda_init)
    else:
        b_out = [_attention_latent(qkv, wts["lam"], wts["diff_norm"], batch, seq, min(1024, seq), lambda_init,
                                   cache, rope_tables)]
    x1 = _merge_project(a_out[0], b_out[0], proj, x2d, mod3, mod_row, wts["w_pa"], wts["w_pb"],
                        wts["w_out"], tm=min(512, span))
    y = _conv_ffn(x1, mod3, mod_row, wts["norm2"], wts["w_up"], wts["conv_w"], wts["conv_b"],
                  wts["w_down"], wts["final_norm"], seq, tm=min(512, span))
    return y, a_out[1:], b_out[1:]


def kernel(x_prompt, x_sample, cache_k, cache_v, state_C, state_n, state_m, c, c_ctx, w_mod, b_mod, norm1, w_in, b_gates, mlstm_norm, lam_q1, lam_k1, lam_q2, lam_k2, diff_norm, w_pa, w_pb, w_out, norm2, w_up, conv_w, conv_b, w_down, final_norm):
    assert w_in.shape[0] == 1, "single trunk layer"
    bp, sp, d = x_prompt.shape
    bs, ss, _ = x_sample.shape
    past = cache_k.shape[3]
    lambda_init = 0.8 - 0.6 * math.exp(-0.3 * 0)

    wa_cols = 2 * H_A * DQK_A + 2 * H_A * DV_A
    wb_cols = 3 * H_B * 2 * DH_B
    g0 = wa_cols
    b0 = g0 + 4 * H_A
    m0 = b0 + wb_cols
    w = w_in.reshape(w_in.shape[1:])
    wts = {
        "w_main": jnp.concatenate([w[:, m0:].astype(BF16), w[:, :g0].astype(BF16),
                                   w[:, b0:m0].astype(BF16)], axis=1),
        "cols16": 2 * d + wa_cols,
        "w_gates": _gate_columns(w[:, g0:b0]).astype(BF16),
        "b_gates": _gate_columns(b_gates.reshape(-1))[None, :],
        "norm1": norm1, "norm2": norm2, "mlstm_norm": mlstm_norm, "diff_norm": diff_norm,
        "lam": jnp.concatenate([lam_q1, lam_k1, lam_q2, lam_k2], axis=0),
        "w_pa": w_pa.reshape(w_pa.shape[1:]).astype(BF16),
        "w_pb": w_pb.reshape(w_pb.shape[1:]).astype(BF16),
        "w_out": w_out.reshape(w_out.shape[1:]).astype(BF16),
        "w_up": w_up.reshape(w_up.shape[1:]).astype(BF16),
        "conv_w": conv_w.reshape(conv_w.shape[1:]), "conv_b": conv_b,
        "w_down": w_down.reshape(w_down.shape[1:]).astype(BF16),
        "final_norm": final_norm[None, :],
    }

    c_all = jnp.concatenate([c_ctx[None, :], c, jnp.zeros((8 - 1 - bs, d), F32)], axis=0)
    mod3 = _modulation(c_all, w_mod.reshape(w_mod.shape[1:]), b_mod).reshape(8, 6, d)

    yp, (c_new, n_new, m_new), (k_new, v_new) = _trunk(
        x_prompt.reshape(bp * sp, d), bp, sp, mod3, lambda t: 0, wts, lambda_init,
        chunk=sp, emit_state=True)

    state = (state_C.reshape(bs, 2, H_A, DQK_A, DV_A), state_n.reshape(bs, 2, H_A // 2, LANES),
             state_m.reshape(bs, 2, H_A))
    cache = (cache_k.reshape(bs, H_B, past, 2 * DH_B), cache_v.reshape(bs, H_B, past, DV_B))
    ys, _, _ = _trunk(
        x_sample.reshape(bs * ss, d), bs, ss, mod3, lambda t: 1 + t // ss, wts, lambda_init,
        chunk=256, cache=cache, state=state, rope_tables=_rope_tables(ss))

    return (yp.reshape(bp, sp, d), ys.reshape(bs, ss, d),
            k_new.reshape(bp, 1, H_B, sp, 2 * DH_B), v_new.reshape(bp, 1, H_B, sp, DV_B),
            c_new.reshape(bp, 1, 2, H_A, DQK_A, DV_A), n_new.reshape(bp, 1, 2, H_A, DQK_A),
            m_new.reshape(bp, 1, 2, H_A))
```
